```python
import math
import jax, jax.numpy as jnp
from jax import lax
import numpy as np

D_MODEL = 4096
BATCH = 1
SEQ = 16384
DEPTH = 4

A_HEADS = 4
A_V_DIM = D_MODEL // 8
A_QK_DIM = A_V_DIM // 2
A_QK = A_HEADS * A_QK_DIM
A_WIDTH = A_HEADS * A_V_DIM
A_CHUNK = 64
CONV_WIDTH = 4
B_HEAD_DIM = 128
B_HEADS = 8
B_WIDTH = B_HEADS * B_HEAD_DIM
Q_BLOCK = 128
POOL_WINDOWS = (2, 4, 8, 16)
C_GROUPS = 4
C_WIDTH = D_MODEL
C_GROUP_DIM = C_WIDTH // C_GROUPS

EVEN_SPLITS = (2 * A_QK, A_WIDTH, A_WIDTH, A_WIDTH, A_HEADS, A_HEADS,
               B_WIDTH, B_WIDTH, B_WIDTH, B_WIDTH)
EVEN_IN = sum(EVEN_SPLITS)
EVEN_MIX = A_WIDTH + B_WIDTH
ODD_IN = 2 * C_WIDTH

DEEPNORM_ALPHA = (2 * DEPTH) ** 0.25
DEEPNORM_BETA = (8 * DEPTH) ** -0.25
LN_EPS = 1e-5

kernel_name = 'hybrid_mlstm_stickbreak_pool_deepnorm'


def layer_norm(x, g, b):
    xf = x.astype(jnp.float32)
    mu = xf.mean(-1, keepdims=True)
    var = jnp.square(xf - mu).mean(-1, keepdims=True)
    return ((xf - mu) * lax.rsqrt(var + LN_EPS) * g + b).astype(x.dtype)


def head_norm(h, w):
    mu = h.mean(-1, keepdims=True)
    var = jnp.square(h - mu).mean(-1, keepdims=True)
    return (h - mu) * lax.rsqrt(var + LN_EPS) * w


def causal_dwconv(x, w):
    K = w.shape[0]
    S = x.shape[1]
    xp = jnp.pad(x, ((0, 0), (K - 1, 0), (0, 0)))
    return sum(w[j] * xp[:, j:j + S] for j in range(K))


def mlstm_chunkwise(q, k, v, log_i, log_f):
    f32 = jnp.float32
    Bsz, S, H, dk = q.shape
    dv = v.shape[-1]
    L = A_CHUNK
    NC = S // L
    q = q.astype(f32) * dk ** -0.5

    def chunks(a):
        return a.astype(f32).reshape(Bsz, NC, L, H, -1).transpose(1, 0, 3, 2, 4)

    def gchunks(a):
        return a.astype(f32).reshape(Bsz, NC, L, H).transpose(1, 0, 3, 2)

    causal = jnp.tril(jnp.ones((L, L), dtype=bool))

    def step(carry, inp):
        C, n, m = carry
        qc, kc, vc, li, lf = inp
        b = jnp.cumsum(lf, axis=-1)
        dmat = jnp.where(causal, b[..., :, None] - b[..., None, :] + li[..., None, :], -jnp.inf)
        inter = b + m[..., None]
        m_t = jnp.maximum(inter, dmat.max(-1))
        w_intra = jnp.exp(dmat - m_t[..., None])
        s_intra = jnp.einsum('bhtd,bhsd->bhts', qc, kc) * w_intra
        w_inter = jnp.exp(inter - m_t)
        num = (jnp.einsum('bhts,bhsv->bhtv', s_intra, vc)
               + w_inter[..., None] * jnp.einsum('bhtd,bhdv->bhtv', qc, C))
        den = s_intra.sum(-1) + w_inter * jnp.einsum('bhtd,bhd->bht', qc, n)
        den = jnp.maximum(jnp.abs(den), jnp.exp(-m_t))
        h = num / den[..., None]
        b_last = b[..., -1]
        wdec = b_last[..., None] - b + li
        m_new = jnp.maximum(b_last + m, wdec.max(-1))
        ws = jnp.exp(wdec - m_new[..., None])
        carry_decay = jnp.exp(b_last + m - m_new)
        C_new = carry_decay[..., None, None] * C + jnp.einsum('bhs,bhsd,bhsv->bhdv', ws, kc, vc)
        n_new = carry_decay[..., None] * n + jnp.einsum('bhs,bhsd->bhd', ws, kc)
        return (C_new, n_new, m_new), h

    init = (jnp.zeros((Bsz, H, dk, dv), f32), jnp.zeros((Bsz, H, dk), f32),
            jnp.zeros((Bsz, H), f32))
    _, h = lax.scan(step, init, (chunks(q), chunks(k), chunks(v), gchunks(log_i), gchunks(log_f)))
    return h.transpose(1, 0, 3, 2, 4).reshape(Bsz, S, H, dv)


def stick_breaking(q, k, v):
    f32 = jnp.float32
    Bsz, S, H, dh = q.shape
    NQB = S // Q_BLOCK
    qt = (q.astype(f32) * dh ** -0.5).transpose(0, 2, 1, 3)
    kt = k.astype(f32).transpose(0, 2, 1, 3)
    vt = v.astype(f32).transpose(0, 2, 1, 3)
    ar = jnp.arange(Q_BLOCK)
    diag_mask = ar[None, :] < ar[:, None]
    outs = []
    for i in range(NQB):
        start, end = i * Q_BLOCK, (i + 1) * Q_BLOCK
        mask = jnp.concatenate([jnp.ones((Q_BLOCK, start), dtype=bool), diag_mask], axis=1)
        z = jnp.einsum('bhqd,bhsd->bhqs', qt[:, :, start:end], kt[:, :, :end])
        ls = jax.nn.log_sigmoid(z)
        log_keep = jnp.where(mask, ls - z, 0.0)
        between = lax.cumsum(log_keep, axis=3, reverse=True) - log_keep
        weights = jnp.where(mask, jnp.exp(ls + between), 0.0)
        outs.append(jnp.einsum('bhqs,bhsd->bhqd', weights, vt[:, :, :end]))
    out = jnp.concatenate(outs, axis=2)
    return out.transpose(0, 2, 1, 3).reshape(Bsz, S, H * dh)


def even_mixer(u, w_in, conv_w, igate_b, fgate_b, head_norm_w, w_out):
    f32 = jnp.float32
    Bsz, S, _ = u.shape
    p = u @ w_in
    idx = np.cumsum(EVEN_SPLITS)[:-1].tolist()
    qkA, vA, oA, zA, iA, fA, qB, kB, vB, zB = jnp.split(p, idx, axis=-1)
    qkA = jax.nn.silu(causal_dwconv(qkA, conv_w))
    qA, kA = jnp.split(qkA, 2, axis=-1)
    log_i = iA.astype(f32) + igate_b
    log_f = jax.nn.log_sigmoid(fA.astype(f32) + fgate_b)
    hA = mlstm_chunkwise(qA.reshape(Bsz, S, A_HEADS, A_QK_DIM),
                         kA.reshape(Bsz, S, A_HEADS, A_QK_DIM),
                         vA.reshape(Bsz, S, A_HEADS, A_V_DIM), log_i, log_f)
    hA = jax.nn.sigmoid(oA.astype(f32)).reshape(Bsz, S, A_HEADS, A_V_DIM) * hA
    hA = head_norm(hA, head_norm_w.reshape(A_HEADS, A_V_DIM)).reshape(Bsz, S, A_WIDTH)
    hA = hA * jax.nn.silu(zA.astype(f32))
    hB = stick_breaking(qB.reshape(Bsz, S, B_HEADS, B_HEAD_DIM),
                        kB.reshape(Bsz, S, B_HEADS, B_HEAD_DIM),
                        vB.reshape(Bsz, S, B_HEADS, B_HEAD_DIM))
    hB = hB * jax.nn.silu(zB.astype(f32))
    mix = jnp.concatenate([hA, hB], axis=-1).astype(u.dtype)
    return mix @ w_out


def pool_mixer(u, w_in, pool_w, pool_b, pool_scale, w_out):
    f32 = jnp.float32
    Bsz, S, _ = u.shape
    p = u @ w_in
    v, z = jnp.split(p, 2, axis=-1)
    vg = v.astype(f32).reshape(Bsz, S, C_GROUPS, C_GROUP_DIM)
    csum = jnp.concatenate([jnp.zeros_like(vg[:, :1]), jnp.cumsum(vg, axis=1)], axis=1)
    pos = jnp.arange(1, S + 1, dtype=f32)
    groups = []
    for g, w in enumerate(POOL_WINDOWS):
        cg = csum[:, :, g]
        lag = jnp.concatenate([jnp.zeros_like(cg[:, :w - 1]), cg[:, :S + 1 - w]], axis=1)
        mean = (cg[:, 1:] - lag) / jnp.minimum(pos, w)[None, :, None]
        groups.append(mean - vg[:, :, g])
    pooled = jnp.stack(groups, axis=2)
    mixed = jnp.einsum('bsgc,gcd->bsgd', pooled, pool_w.astype(f32)) + pool_b
    h = mixed.reshape(Bsz, S, C_WIDTH) * pool_scale * jax.nn.silu(z.astype(f32))
    return h.astype(u.dtype) @ w_out


def _even_params(ks, l):
    D = D_MODEL
    return {
        'ada_w_%d' % l: jax.random.normal(next(ks), (D, 3 * D)) * (0.2 * D ** -0.5),
        'ada_b_%d' % l: jax.random.normal(next(ks), (3 * D,)) * 0.01,
        'w_in_%d' % l: jax.random.normal(next(ks), (D, EVEN_IN)) * D ** -0.5,
        'conv_w_%d' % l: jax.random.normal(next(ks), (CONV_WIDTH, 2 * A_QK)) * CONV_WIDTH ** -0.5,
        'igate_b_%d' % l: jax.random.normal(next(ks), (A_HEADS,)) * 0.1,
        'fgate_b_%d' % l: 3.0 + jnp.linspace(0.0, 3.0, A_HEADS) + jax.random.normal(next(ks), (A_HEADS,)) * 0.1,
        'head_norm_w_%d' % l: 1.0 + jax.random.normal(next(ks), (A_WIDTH,)) * 0.02,
        'w_out_%d' % l: jax.random.normal(next(ks), (EVEN_MIX, D)) * (EVEN_MIX ** -0.5 * DEEPNORM_BETA),
        'ln_g_%d' % l: 1.0 + jax.random.normal(next(ks), (D,)) * 0.02,
        'ln_b_%d' % l: jax.random.normal(next(ks), (D,)) * 0.02,
    }


def _odd_params(ks, l):
    D = D_MODEL
    return {
        'ada_w_%d' % l: jax.random.normal(next(ks), (D, 3 * D)) * (0.2 * D ** -0.5),
        'ada_b_%d' % l: jax.random.normal(next(ks), (3 * D,)) * 0.01,
        'w_in_%d' % l: jax.random.normal(next(ks), (D, ODD_IN)) * D ** -0.5,
        'pool_w_%d' % l: jax.random.normal(next(ks), (C_GROUPS, C_GROUP_DIM, C_GROUP_DIM)) * C_GROUP_DIM ** -0.5,
        'pool_b_%d' % l: jax.random.normal(next(ks), (C_GROUPS, C_GROUP_DIM)) * 0.02,
        'pool_scale_%d' % l: 1.0 + jax.random.normal(next(ks), (C_WIDTH,)) * 0.1,
        'w_out_%d' % l: jax.random.normal(next(ks), (C_WIDTH, D)) * (C_WIDTH ** -0.5 * DEEPNORM_BETA),
        'ln_g_%d' % l: 1.0 + jax.random.normal(next(ks), (D,)) * 0.02,
        'ln_b_%d' % l: jax.random.normal(next(ks), (D,)) * 0.02,
    }


def setup_inputs(seed: int = 0) -> dict:
    key = jax.random.key(seed)
    ks = iter(jax.random.split(key, 64))
    params = {
        'x': jax.random.normal(next(ks), (BATCH, SEQ, D_MODEL), dtype=jnp.float32),
        'c': jax.random.normal(next(ks), (BATCH, D_MODEL), dtype=jnp.float32),
    }
    for l in range(DEPTH):
        params.update(_even_params(ks, l) if l % 2 == 0 else _odd_params(ks, l))
    return params


def reference(x, c,
              ada_w_0, ada_b_0, w_in_0, conv_w_0, igate_b_0, fgate_b_0, head_norm_w_0, w_out_0, ln_g_0, ln_b_0,
              ada_w_1, ada_b_1, w_in_1, pool_w_1, pool_b_1, pool_scale_1, w_out_1, ln_g_1, ln_b_1,
              ada_w_2, ada_b_2, w_in_2, conv_w_2, igate_b_2, fgate_b_2, head_norm_w_2, w_out_2, ln_g_2, ln_b_2,
              ada_w_3, ada_b_3, w_in_3, pool_w_3, pool_b_3, pool_scale_3, w_out_3, ln_g_3, ln_b_3):
    ada = [(ada_w_0, ada_b_0), (ada_w_1, ada_b_1), (ada_w_2, ada_b_2), (ada_w_3, ada_b_3)]
    post = [(ln_g_0, ln_b_0), (ln_g_1, ln_b_1), (ln_g_2, ln_b_2), (ln_g_3, ln_b_3)]
    mixers = [
        (w_in_0, conv_w_0, igate_b_0, fgate_b_0, head_norm_w_0, w_out_0),
        (w_in_1, pool_w_1, pool_b_1, pool_scale_1, w_out_1),
        (w_in_2, conv_w_2, igate_b_2, fgate_b_2, head_norm_w_2, w_out_2),
        (w_in_3, pool_w_3, pool_b_3, pool_scale_3, w_out_3),
    ]
    c_act = jax.nn.silu(c)
    for l in range(DEPTH):
        mod = c_act @ ada[l][0] + ada[l][1]
        shift, scale, gate = jnp.split(mod, 3, axis=-1)
        u = x * (1.0 + scale[:, None, :]) + shift[:, None, :]
        if l % 2 == 0:
            y = even_mixer(u, *mixers[l])
        else:
            y = pool_mixer(u, *mixers[l])
        x = layer_norm(DEEPNORM_ALPHA * x + (1.0 + gate[:, None, :]) * y, *post[l])
    return x
```

```python
import functools

import jax
import jax.numpy as jnp
from jax import lax
from jax.experimental import pallas as pl
from jax.experimental.pallas import tpu as pltpu

F32 = jnp.float32
BF16 = jnp.bfloat16

DEPTH = 4
A_HEADS = 4
A_QK_DIM = 256
A_V_DIM = 512
A_QK = A_HEADS * A_QK_DIM
A_WIDTH = A_HEADS * A_V_DIM
CONV_WIDTH = 4
B_HEADS = 8
B_HEAD_DIM = 128
B_WIDTH = B_HEADS * B_HEAD_DIM
POOL_WINDOWS = (2, 4, 8, 16)
C_GROUPS = 4
C_GROUP_DIM = 1024
GATE_COL0 = 2 * A_QK + 3 * A_WIDTH
GATE_PAD = 128
DEEPNORM_ALPHA = (2 * DEPTH) ** 0.25
LN_EPS = 1e-5
SOFTPLUS_LINEAR_ABOVE = 20.0

LANE = 128
HALO = 16
VMEM_LIMIT = 56 * 1024 * 1024


def _cparams(sem):
    return pltpu.CompilerParams(dimension_semantics=sem, vmem_limit_bytes=VMEM_LIMIT)


def _sigmoid(x):
    return 1.0 / (1.0 + jnp.exp(-x))


def _silu(x):
    return x * _sigmoid(x)


def _log_sigmoid(x):
    return jnp.minimum(x, 0.0) - jnp.log1p(jnp.exp(-jnp.abs(x)))


def _split_bf16(x, parts):
    out = []
    r = x
    for _ in range(parts - 1):
        h = r.astype(BF16)
        out.append(h)
        r = r - h.astype(F32)
    out.append(r.astype(BF16))
    return out


def _mod_kernel(c_ref, w_ref, b_ref, o_ref):
    c = c_ref[...]
    o_ref[...] = jnp.sum(_silu(c) * w_ref[...], axis=0, keepdims=True) + b_ref[...]


def _modulation(c_col, w, b):
    D, N = w.shape
    tn = 512
    return pl.pallas_call(
        _mod_kernel,
        grid=(N // tn,),
        in_specs=[pl.BlockSpec((D, 1), lambda j: (0, 0)),
                  pl.BlockSpec((D, tn), lambda j: (0, j)),
                  pl.BlockSpec((1, tn), lambda j: (0, j))],
        out_specs=pl.BlockSpec((1, tn), lambda j: (0, j)),
        out_shape=jax.ShapeDtypeStruct((1, N), F32),
        compiler_params=_cparams(("arbitrary",)),
        name="modulation",
    )(c_col, w, b.reshape(1, N))


def _modulate_kernel(x_ref, sh_ref, sc_ref, u_ref):
    u_ref[...] = (x_ref[...] * (1.0 + sc_ref[...]) + sh_ref[...]).astype(BF16)


def _modulate(x, mod):
    S, D = x.shape
    tm = min(512, S)
    return pl.pallas_call(
        _modulate_kernel,
        grid=(S // tm,),
        in_specs=[pl.BlockSpec((tm, D), lambda i: (i, 0)),
                  pl.BlockSpec((1, D), lambda i: (0, 0)),
                  pl.BlockSpec((1, D), lambda i: (0, 1))],
        out_specs=pl.BlockSpec((tm, D), lambda i: (i, 0)),
        out_shape=jax.ShapeDtypeStruct((S, D), BF16),
        compiler_params=_cparams(("arbitrary",)),
        name="modulate",
    )(x, mod, mod)


def _mm_kernel(a_ref, b_ref, s_ref, o_ref):
    acc = jnp.dot(a_ref[...], b_ref[...], preferred_element_type=F32)
    o_ref[...] = (acc * s_ref[...]).astype(o_ref.dtype)


def _matmul(a, b, col_scale, out_dtype, *, tm, tn, name):
    M, K = a.shape
    N = b.shape[1]
    tm, tn = min(tm, M), min(tn, N)
    return pl.pallas_call(
        _mm_kernel,
        grid=(M // tm, N // tn),
        in_specs=[pl.BlockSpec((tm, K), lambda i, j: (i, 0)),
                  pl.BlockSpec((K, tn), lambda i, j: (0, j)),
                  pl.BlockSpec((1, tn), lambda i, j: (0, j))],
        out_specs=pl.BlockSpec((tm, tn), lambda i, j: (i, j)),
        out_shape=jax.ShapeDtypeStruct((M, N), out_dtype),
        compiler_params=_cparams(("arbitrary", "arbitrary")),
        name=name,
    )(a, b, col_scale)


def _mm2_kernel(a1_ref, b1_ref, a2_ref, b2_ref, o_ref):
    o_ref[...] = (jnp.dot(a1_ref[...], b1_ref[...], preferred_element_type=F32)
                  + jnp.dot(a2_ref[...], b2_ref[...], preferred_element_type=F32))


def _matmul2(a1, b1, a2, b2, *, tm, tn, name):
    M, K1 = a1.shape
    K2 = a2.shape[1]
    N = b1.shape[1]
    tm, tn = min(tm, M), min(tn, N)
    return pl.pallas_call(
        _mm2_kernel,
        grid=(M // tm, N // tn),
        in_specs=[pl.BlockSpec((tm, K1), lambda i, j: (i, 0)),
                  pl.BlockSpec((K1, tn), lambda i, j: (0, j)),
                  pl.BlockSpec((tm, K2), lambda i, j: (i, 0)),
                  pl.BlockSpec((K2, tn), lambda i, j: (0, j))],
        out_specs=pl.BlockSpec((tm, tn), lambda i, j: (i, j)),
        out_shape=jax.ShapeDtypeStruct((M, N), F32),
        compiler_params=_cparams(("arbitrary", "arbitrary")),
        name=name,
    )(a1, b1, a2, b2)


def _ln_kernel(x_ref, y_ref, gate_ref, g_ref, b_ref, *rest, with_next):
    v = DEEPNORM_ALPHA * x_ref[...] + (1.0 + gate_ref[...]) * y_ref[...]
    mu = jnp.mean(v, axis=-1, keepdims=True)
    d = v - mu
    var = jnp.mean(d * d, axis=-1, keepdims=True)
    xn = d * lax.rsqrt(var + LN_EPS) * g_ref[...] + b_ref[...]
    if with_next:
        sh_ref, sc_ref, xo_ref, u_ref = rest
        u_ref[...] = (xn * (1.0 + sc_ref[...]) + sh_ref[...]).astype(BF16)
    else:
        (xo_ref,) = rest
    xo_ref[...] = xn


def _residual_ln(x, y, mod, ln_g, ln_b, mod_next):
    S, D = x.shape
    tm = min(256, S)
    row = pl.BlockSpec((tm, D), lambda i: (i, 0))

    def vec(k):
        return pl.BlockSpec((1, D), lambda i: (0, k))

    in_specs = [row, row, vec(2), vec(0), vec(0)]
    args = [x, y, mod, ln_g.reshape(1, D), ln_b.reshape(1, D)]
    out_specs = [row]
    out_shape = [jax.ShapeDtypeStruct((S, D), F32)]
    if mod_next is not None:
        in_specs += [vec(0), vec(1)]
        args += [mod_next, mod_next]
        out_specs.append(row)
        out_shape.append(jax.ShapeDtypeStruct((S, D), BF16))
    res = pl.pallas_call(
        functools.partial(_ln_kernel, with_next=mod_next is not None),
        grid=(S // tm,),
        in_specs=in_specs,
        out_specs=out_specs,
        out_shape=out_shape,
        compiler_params=_cparams(("arbitrary",)),
        name="residual_ln",
    )(*args)
    return (res[0], res[1]) if mod_next is not None else (res[0], None)


def _mlstm_kernel(qk_ref, halo_ref, v_ref, o_ref, z_ref, g_ref, conv_ref, gb_ref, hn_ref,
                  out_ref, C_ref, n_ref, m_ref, *, L):
    H, DK, DV = A_HEADS, A_QK_DIM, A_V_DIM
    c = pl.program_id(0)

    @pl.when(c == 0)
    def _():
        C_ref[...] = jnp.zeros_like(C_ref)
        n_ref[...] = jnp.zeros_like(n_ref)
        m_ref[...] = jnp.zeros_like(m_ref)

    cur = qk_ref[...].astype(F32)
    halo = jnp.where(c > 0, halo_ref[...].astype(F32), 0.0)
    xp = jnp.concatenate([halo, cur], axis=0)
    cw = conv_ref[...]
    acc = cw[CONV_WIDTH - 1:CONV_WIDTH] * cur
    for j in range(CONV_WIDTH - 1):
        shifted = pltpu.roll(xp, CONV_WIDTH - 1 - j, axis=0)[HALO:]
        acc = acc + cw[j:j + 1] * shifted
    qk = _silu(acc)

    gb = g_ref[...] + gb_ref[...]
    lane = lax.broadcasted_iota(jnp.int32, gb.shape, 1)
    gcol = jnp.where(lane < H, gb, _log_sigmoid(gb))
    grow = gcol.T[:2 * H]
    r_i = lax.broadcasted_iota(jnp.int32, (L, L), 0)
    c_i = lax.broadcasted_iota(jnp.int32, (L, L), 1)
    causal = c_i <= r_i
    tri_lo = jnp.where(causal, 1.0, 0.0).astype(BF16)
    tri_up = jnp.where(c_i >= r_i, 1.0, 0.0).astype(BF16)
    bcol = sum(jnp.dot(tri_lo, p, preferred_element_type=F32) for p in _split_bf16(gcol, 3))
    brow = sum(jnp.dot(p, tri_up, preferred_element_type=F32) for p in _split_bf16(grow, 3))

    for h in range(H):
        li_row = grow[h:h + 1, :]
        li_col = gcol[:, h:h + 1]
        b_row = brow[H + h:H + h + 1, :]
        b_col = bcol[:, H + h:H + h + 1]
        m_prev = m_ref[h][0:1, 0:1]
        q = qk[:, h * DK:(h + 1) * DK] * (DK ** -0.5)
        k = qk[:, (H + h) * DK:(H + h + 1) * DK]
        vh = v_ref[:, h * DV:(h + 1) * DV]
        qb = q.astype(BF16)

        dmat = jnp.where(causal, b_col - b_row + li_row, -jnp.inf)
        inter = b_col + m_prev
        m_t = jnp.maximum(inter, jnp.max(dmat, axis=1, keepdims=True))
        w_intra = jnp.exp(dmat - m_t)
        s = lax.dot_general(qb, k.astype(BF16), (((1,), (1,)), ((), ())),
                            preferred_element_type=F32) * w_intra
        w_inter = jnp.exp(inter - m_t)
        num = (jnp.dot(s.astype(BF16), vh, preferred_element_type=F32)
               + w_inter * jnp.dot(qb, C_ref[h].astype(BF16), preferred_element_type=F32))
        den = (jnp.sum(s, axis=1, keepdims=True)
               + w_inter * jnp.sum(q * n_ref[h], axis=1, keepdims=True))
        den = jnp.maximum(jnp.abs(den), jnp.exp(-m_t))
        hh = num / den

        b_last = b_col[L - 1:L, :]
        wdec_row = b_last - b_row + li_row
        wdec_col = b_last - b_col + li_col
        m_new = jnp.maximum(b_last + m_prev, jnp.max(wdec_row, axis=1, keepdims=True))
        kw = k * jnp.exp(wdec_col - m_new)
        decay = jnp.exp(b_last + m_prev - m_new)
        C_ref[h] = decay * C_ref[h] + lax.dot_general(
            kw.astype(BF16), vh, (((0,), (0,)), ((), ())), preferred_element_type=F32)
        n_ref[h] = decay * n_ref[h] + jnp.sum(kw, axis=0, keepdims=True)
        m_ref[h] = jnp.broadcast_to(m_new, m_ref.shape[1:])

        og = _sigmoid(o_ref[:, h * DV:(h + 1) * DV].astype(F32)) * hh
        mu = jnp.mean(og, axis=-1, keepdims=True)
        d = og - mu
        var = jnp.mean(d * d, axis=-1, keepdims=True)
        hn = d * lax.rsqrt(var + LN_EPS) * hn_ref[:, h * DV:(h + 1) * DV]
        out_ref[:, h * DV:(h + 1) * DV] = (
            hn * _silu(z_ref[:, h * DV:(h + 1) * DV].astype(F32))).astype(BF16)


def _mlstm(p, gates, conv_w, gate_bias, head_norm_w, *, L):
    S = p.shape[0]
    L = min(L, S)
    QK2 = 2 * A_QK
    blk = lambda col: pl.BlockSpec((L, A_WIDTH), lambda c: (c, col))
    return pl.pallas_call(
        functools.partial(_mlstm_kernel, L=L),
        grid=(S // L,),
        in_specs=[pl.BlockSpec((L, QK2), lambda c: (c, 0)),
                  pl.BlockSpec((HALO, QK2), lambda c: (jnp.maximum(c * (L // HALO) - 1, 0), 0)),
                  blk(1), blk(2), blk(3),
                  pl.BlockSpec((L, GATE_PAD), lambda c: (c, 0)),
                  pl.BlockSpec((CONV_WIDTH, QK2), lambda c: (0, 0)),
                  pl.BlockSpec((1, GATE_PAD), lambda c: (0, 0)),
                  pl.BlockSpec((1, A_WIDTH), lambda c: (0, 0))],
        out_specs=pl.BlockSpec((L, A_WIDTH), lambda c: (c, 0)),
        out_shape=jax.ShapeDtypeStruct((S, A_WIDTH), BF16),
        scratch_shapes=[pltpu.VMEM((A_HEADS, A_QK_DIM, A_V_DIM), F32),
                        pltpu.VMEM((A_HEADS, 1, A_QK_DIM), F32),
                        pltpu.VMEM((A_HEADS, 8, LANE), F32)],
        compiler_params=_cparams(("arbitrary",)),
        name="mlstm",
    )(p, p, p, p, p, gates, conv_w, gate_bias, head_norm_w.reshape(1, A_WIDTH))


def _attn_kernel(q_ref, k_ref, v_ref, z_ref, o_ref, *, T):
    i = pl.program_id(1)
    q = q_ref[...]
    r_i = lax.broadcasted_iota(jnp.int32, (T, T), 0)
    c_i = lax.broadcasted_iota(jnp.int32, (T, T), 1)
    tri = jnp.where(r_i >= c_i, 1.0, 0.0).astype(BF16)
    strict = c_i < r_i

    def block(start, carry, acc, diag):
        kb = k_ref[pl.ds(start, T), :]
        vb = v_ref[pl.ds(start, T), :]
        z = lax.dot_general(q, kb, (((1,), (1,)), ((), ())), preferred_element_type=F32)
        sp = jnp.where(z > SOFTPLUS_LINEAR_ABOVE, z, jnp.log1p(jnp.exp(z)))
        if diag:
            sp = jnp.where(strict, sp, 0.0)
        hi, lo = _split_bf16(sp, 2)
        rc = (jnp.dot(hi, tri, preferred_element_type=F32)
              + jnp.dot(lo, tri, preferred_element_type=F32))
        p = jnp.exp(z - rc - carry)
        if diag:
            p = jnp.where(strict, p, 0.0)
        acc = acc + jnp.dot(p.astype(BF16), vb, preferred_element_type=F32)
        return carry + rc[:, 0:1], acc

    carry0 = jnp.zeros((T, 1), F32)
    acc0 = jnp.zeros((T, B_HEAD_DIM), F32)
    carry, acc = block(pl.multiple_of(i * T, T), carry0, acc0, True)

    def body(jj, ca):
        start = pl.multiple_of((i - 1 - jj) * T, T)
        return block(start, ca[0], ca[1], False)

    carry, acc = lax.fori_loop(0, i, body, (carry, acc))
    o_ref[...] = (acc * _silu(z_ref[...].astype(F32))).astype(BF16)


def _stick_breaking(p, *, T):
    S = p.shape[0]
    T = min(T, S)
    c0 = (2 * A_QK + 3 * A_WIDTH) // B_HEAD_DIM
    nb = B_WIDTH // B_HEAD_DIM
    return pl.pallas_call(
        functools.partial(_attn_kernel, T=T),
        grid=(B_HEADS, S // T),
        in_specs=[pl.BlockSpec((T, B_HEAD_DIM), lambda h, i: (i, c0 + h)),
                  pl.BlockSpec((S, B_HEAD_DIM), lambda h, i: (0, c0 + nb + h)),
                  pl.BlockSpec((S, B_HEAD_DIM), lambda h, i: (0, c0 + 2 * nb + h)),
                  pl.BlockSpec((T, B_HEAD_DIM), lambda h, i: (i, c0 + 3 * nb + h))],
        out_specs=pl.BlockSpec((T, B_HEAD_DIM), lambda h, i: (i, h)),
        out_shape=jax.ShapeDtypeStruct((S, B_WIDTH), BF16),
        compiler_params=_cparams(("arbitrary", "arbitrary")),
        name="stick_breaking",
    )(p, p, p, p)


def _pool_kernel(v_ref, halo_ref, z_ref, w_ref, b_ref, s_ref, o_ref, *, tm):
    i = pl.program_id(0)
    GD = C_GROUP_DIM
    pos = (lax.broadcasted_iota(jnp.int32, (tm, 1), 0) + i * tm + 1).astype(F32)
    for g, win in enumerate(POOL_WINDOWS):
        cur = v_ref[:, g * GD:(g + 1) * GD].astype(F32)
        halo = jnp.where(i > 0, halo_ref[:, g * GD:(g + 1) * GD].astype(F32), 0.0)
        a = jnp.concatenate([halo, cur], axis=0)
        step = 1
        while step < win:
            a = a + pltpu.roll(a, step, axis=0)
            step *= 2
        pooled = a[HALO:] / jnp.minimum(pos, float(win)) - cur
        mixed = jnp.dot(pooled.astype(BF16), w_ref[g], preferred_element_type=F32) + b_ref[g]
        gate = _silu(z_ref[:, g * GD:(g + 1) * GD].astype(F32))
        o_ref[:, g * GD:(g + 1) * GD] = (mixed * s_ref[:, g * GD:(g + 1) * GD] * gate).astype(BF16)


def _pool_mixer(p, pool_w, pool_b, pool_scale, *, tm):
    S = p.shape[0]
    tm = min(tm, S)
    W = C_GROUPS * C_GROUP_DIM
    return pl.pallas_call(
        functools.partial(_pool_kernel, tm=tm),
        grid=(S // tm,),
        in_specs=[pl.BlockSpec((tm, W), lambda i: (i, 0)),
                  pl.BlockSpec((HALO, W), lambda i: (jnp.maximum(i * (tm // HALO) - 1, 0), 0)),
                  pl.BlockSpec((tm, W), lambda i: (i, 1)),
                  pl.BlockSpec((C_GROUPS, C_GROUP_DIM, C_GROUP_DIM), lambda i: (0, 0, 0)),
                  pl.BlockSpec((C_GROUPS, 1, C_GROUP_DIM), lambda i: (0, 0, 0)),
                  pl.BlockSpec((1, W), lambda i: (0, 0))],
        out_specs=pl.BlockSpec((tm, W), lambda i: (i, 0)),
        out_shape=jax.ShapeDtypeStruct((S, W), BF16),
        compiler_params=_cparams(("arbitrary",)),
        name="pool_mixer",
    )(p, p, p, pool_w.astype(BF16), pool_b.reshape(C_GROUPS, 1, C_GROUP_DIM), pool_scale.reshape(1, W))


def _even_layer(u, w_in, conv_w, igate_b, fgate_b, head_norm_w, w_out):
    D = w_in.shape[0]
    w_main = jnp.concatenate([w_in[:, :GATE_COL0], w_in[:, GATE_COL0 + 2 * A_HEADS:]], axis=1).astype(BF16)
    w_gate = jnp.pad(w_in[:, GATE_COL0:GATE_COL0 + 2 * A_HEADS],
                     ((0, 0), (0, GATE_PAD - 2 * A_HEADS))).astype(BF16)
    n_main = w_main.shape[1]
    col = jnp.arange(n_main)
    col_scale = jnp.where((col >= GATE_COL0) & (col < GATE_COL0 + B_WIDTH),
                          B_HEAD_DIM ** -0.5, 1.0).astype(F32).reshape(1, n_main)
    p = _matmul(u, w_main, col_scale, BF16, tm=1024, tn=1024, name="in_proj_even")
    gates = _matmul(u, w_gate, jnp.ones((1, GATE_PAD), F32), F32, tm=1024, tn=GATE_PAD, name="gate_proj")
    gate_bias = jnp.pad(jnp.concatenate([igate_b, fgate_b]), (0, GATE_PAD - 2 * A_HEADS)).reshape(1, GATE_PAD)
    mix_a = _mlstm(p, gates, conv_w, gate_bias, head_norm_w, L=256)
    mix_b = _stick_breaking(p, T=256)
    w_out = w_out.astype(BF16)
    return _matmul2(mix_a, w_out[:A_WIDTH], mix_b, w_out[A_WIDTH:], tm=1024, tn=1024, name="out_proj_even")


def _odd_layer(u, w_in, pool_w, pool_b, pool_scale, w_out):
    D = w_in.shape[0]
    n_in = w_in.shape[1]
    p = _matmul(u, w_in.astype(BF16), jnp.ones((1, n_in), F32), BF16, tm=1024, tn=1024, name="in_proj_odd")
    h = _pool_mixer(p, pool_w, pool_b, pool_scale, tm=256)
    return _matmul(h, w_out.astype(BF16), jnp.ones((1, D), F32), F32, tm=1024, tn=1024, name="out_proj_odd")


def kernel(x, c, ada_w_0, ada_b_0, w_in_0, conv_w_0, igate_b_0, fgate_b_0, head_norm_w_0, w_out_0, ln_g_0, ln_b_0, ada_w_1, ada_b_1, w_in_1, pool_w_1, pool_b_1, pool_scale_1, w_out_1, ln_g_1, ln_b_1, ada_w_2, ada_b_2, w_in_2, conv_w_2, igate_b_2, fgate_b_2, head_norm_w_2, w_out_2, ln_g_2, ln_b_2, ada_w_3, ada_b_3, w_in_3, pool_w_3, pool_b_3, pool_scale_3, w_out_3, ln_g_3, ln_b_3):
    B, S, D = x.shape
    assert B == 1
    ada = [(ada_w_0, ada_b_0), (ada_w_1, ada_b_1), (ada_w_2, ada_b_2), (ada_w_3, ada_b_3)]
    post = [(ln_g_0, ln_b_0), (ln_g_1, ln_b_1), (ln_g_2, ln_b_2), (ln_g_3, ln_b_3)]
    mixers = [
        (w_in_0, conv_w_0, igate_b_0, fgate_b_0, head_norm_w_0, w_out_0),
        (w_in_1, pool_w_1, pool_b_1, pool_scale_1, w_out_1),
        (w_in_2, conv_w_2, igate_b_2, fgate_b_2, head_norm_w_2, w_out_2),
        (w_in_3, pool_w_3, pool_b_3, pool_scale_3, w_out_3),
    ]
    c_col = c.reshape(D, 1)
    mods = [_modulation(c_col, w, b) for (w, b) in ada]
    xs = x.reshape(S, D)
    u = _modulate(xs, mods[0])
    for l in range(DEPTH):
        y = _even_layer(u, *mixers[l]) if l % 2 == 0 else _odd_layer(u, *mixers[l])
        xs, u = _residual_ln(xs, y, mods[l], *post[l], mods[l + 1] if l + 1 < DEPTH else None)
    return xs.reshape(B, S, D)
```

```python
import functools

import jax
import jax.numpy as jnp
from jax import lax
from jax.experimental import pallas as pl
from jax.experimental.pallas import tpu as pltpu

F32 = jnp.float32
BF16 = jnp.bfloat16

DEPTH = 4
A_HEADS = 4
A_QK_DIM = 256
A_V_DIM = 512
A_QK = A_HEADS * A_QK_DIM
A_WIDTH = A_HEADS * A_V_DIM
CONV_WIDTH = 4
B_HEADS = 8
B_HEAD_DIM = 128
B_WIDTH = B_HEADS * B_HEAD_DIM
POOL_WINDOWS = (2, 4, 8, 16)
C_GROUPS = 4
C_GROUP_DIM = 1024
GATE_COL0 = 2 * A_QK + 3 * A_WIDTH
GATE_PAD = 128
DEEPNORM_ALPHA = (2 * DEPTH) ** 0.25
LN_EPS = 1e-5
SOFTPLUS_LINEAR_ABOVE = 20.0

LANE = 128
HALO = 16
VMEM_LIMIT = 56 * 1024 * 1024


def _cparams(sem):
    return pltpu.CompilerParams(dimension_semantics=sem, vmem_limit_bytes=VMEM_LIMIT)


def _sigmoid(x):
    return 1.0 / (1.0 + jnp.exp(-x))


def _silu(x):
    return x * _sigmoid(x)


def _log_sigmoid(x):
    return jnp.minimum(x, 0.0) - jnp.log1p(jnp.exp(-jnp.abs(x)))


def _split_bf16(x, parts):
    out = []
    r = x
    for _ in range(parts - 1):
        h = r.astype(BF16)
        out.append(h)
        r = r - h.astype(F32)
    out.append(r.astype(BF16))
    return out


def _mod_kernel(c_ref, w_ref, b_ref, o_ref):
    c = c_ref[...]
    o_ref[...] = jnp.sum(_silu(c) * w_ref[...], axis=0, keepdims=True) + b_ref[...]


def _modulation(c_col, w, b):
    D, N = w.shape
    tn = 512
    return pl.pallas_call(
        _mod_kernel,
        grid=(N // tn,),
        in_specs=[pl.BlockSpec((D, 1), lambda j: (0, 0)),
                  pl.BlockSpec((D, tn), lambda j: (0, j)),
                  pl.BlockSpec((1, tn), lambda j: (0, j))],
        out_specs=pl.BlockSpec((1, tn), lambda j: (0, j)),
        out_shape=jax.ShapeDtypeStruct((1, N), F32),
        compiler_params=_cparams(("arbitrary",)),
        name="modulation",
    )(c_col, w, b.reshape(1, N))


def _modulate_kernel(x_ref, sh_ref, sc_ref, u_ref):
    u_ref[...] = (x_ref[...] * (1.0 + sc_ref[...]) + sh_ref[...]).astype(BF16)


def _modulate(x, mod):
    S, D = x.shape
    tm = min(512, S)
    return pl.pallas_call(
        _modulate_kernel,
        grid=(S // tm,),
        in_specs=[pl.BlockSpec((tm, D), lambda i: (i, 0)),
                  pl.BlockSpec((1, D), lambda i: (0, 0)),
                  pl.BlockSpec((1, D), lambda i: (0, 1))],
        out_specs=pl.BlockSpec((tm, D), lambda i: (i, 0)),
        out_shape=jax.ShapeDtypeStruct((S, D), BF16),
        compiler_params=_cparams(("arbitrary",)),
        name="modulate",
    )(x, mod, mod)


def _mm_kernel(a_ref, b_ref, s_ref, o_ref):
    acc = jnp.dot(a_ref[...], b_ref[...], preferred_element_type=F32)
    o_ref[...] = (acc * s_ref[...]).astype(o_ref.dtype)


def _matmul(a, b, col_scale, out_dtype, *, tm, tn, name):
    M, K = a.shape
    N = b.shape[1]
    tm, tn = min(tm, M), min(tn, N)
    return pl.pallas_call(
        _mm_kernel,
        grid=(M // tm, N // tn),
        in_specs=[pl.BlockSpec((tm, K), lambda i, j: (i, 0)),
                  pl.BlockSpec((K, tn), lambda i, j: (0, j)),
                  pl.BlockSpec((1, tn), lambda i, j: (0, j))],
        out_specs=pl.BlockSpec((tm, tn), lambda i, j: (i, j)),
        out_shape=jax.ShapeDtypeStruct((M, N), out_dtype),
        compiler_params=_cparams(("arbitrary", "arbitrary")),
        name=name,
    )(a, b, col_scale)


def _mm2_kernel(a1_ref, b1_ref, a2_ref, b2_ref, o_ref):
    o_ref[...] = (jnp.dot(a1_ref[...], b1_ref[...], preferred_element_type=F32)
                  + jnp.dot(a2_ref[...], b2_ref[...], preferred_element_type=F32))


def _matmul2(a1, b1, a2, b2, *, tm, tn, name):
    M, K1 = a1.shape
    K2 = a2.shape[1]
    N = b1.shape[1]
    tm, tn = min(tm, M), min(tn, N)
    return pl.pallas_call(
        _mm2_kernel,
        grid=(M // tm, N // tn),
        in_specs=[pl.BlockSpec((tm, K1), lambda i, j: (i, 0)),
                  pl.BlockSpec((K1, tn), lambda i, j: (0, j)),
                  pl.BlockSpec((tm, K2), lambda i, j: (i, 0)),
                  pl.BlockSpec((K2, tn), lambda i, j: (0, j))],
        out_specs=pl.BlockSpec((tm, tn), lambda i, j: (i, j)),
        out_shape=jax.ShapeDtypeStruct((M, N), F32),
        compiler_params=_cparams(("arbitrary", "arbitrary")),
        name=name,
    )(a1, b1, a2, b2)


def _ln_kernel(x_ref, y_ref, gate_ref, g_ref, b_ref, *rest, with_next):
    v = DEEPNORM_ALPHA * x_ref[...] + (1.0 + gate_ref[...]) * y_ref[...]
    mu = jnp.mean(v, axis=-1, keepdims=True)
    d = v - mu
    var = jnp.mean(d * d, axis=-1, keepdims=True)
    xn = d * lax.rsqrt(var + LN_EPS) * g_ref[...] + b_ref[...]
    if with_next:
        sh_ref, sc_ref, xo_ref, u_ref = rest
        u_ref[...] = (xn * (1.0 + sc_ref[...]) + sh_ref[...]).astype(BF16)
    else:
        (xo_ref,) = rest
    xo_ref[...] = xn


def _residual_ln(x, y, mod, ln_g, ln_b, mod_next):
    S, D = x.shape
    tm = min(256, S)
    row = pl.BlockSpec((tm, D), lambda i: (i, 0))

    def vec(k):
        return pl.BlockSpec((1, D), lambda i: (0, k))

    in_specs = [row, row, vec(2), vec(0), vec(0)]
    args = [x, y, mod, ln_g.reshape(1, D), ln_b.reshape(1, D)]
    out_specs = [row]
    out_shape = [jax.ShapeDtypeStruct((S, D), F32)]
    if mod_next is not None:
        in_specs += [vec(0), vec(1)]
        args += [mod_next, mod_next]
        out_specs.append(row)
        out_shape.append(jax.ShapeDtypeStruct((S, D), BF16))
    res = pl.pallas_call(
        functools.partial(_ln_kernel, with_next=mod_next is not None),
        grid=(S // tm,),
        in_specs=in_specs,
        out_specs=out_specs,
        out_shape=out_shape,
        compiler_params=_cparams(("arbitrary",)),
        name="residual_ln",
    )(*args)
    return (res[0], res[1]) if mod_next is not None else (res[0], None)


def _mlstm_kernel(qk_ref, halo_ref, v_ref, o_ref, z_ref, g_ref, conv_ref, gb_ref, hn_ref,
                  out_ref, C_ref, n_ref, m_ref, *, L):
    H, DK, DV = A_HEADS, A_QK_DIM, A_V_DIM
    c = pl.program_id(0)

    @pl.when(c == 0)
    def _():
        C_ref[...] = jnp.zeros_like(C_ref)
        n_ref[...] = jnp.zeros_like(n_ref)
        m_ref[...] = jnp.zeros_like(m_ref)

    cur = qk_ref[...].astype(F32)
    halo = jnp.where(c > 0, halo_ref[...].astype(F32), 0.0)
    xp = jnp.concatenate([halo, cur], axis=0)
    cw = conv_ref[...]
    acc = cw[CONV_WIDTH - 1:CONV_WIDTH] * cur
    for j in range(CONV_WIDTH - 1):
        shifted = pltpu.roll(xp, CONV_WIDTH - 1 - j, axis=0)[HALO:]
        acc = acc + cw[j:j + 1] * shifted
    qk = _silu(acc)

    gb = g_ref[...] + gb_ref[...]
    lane = lax.broadcasted_iota(jnp.int32, gb.shape, 1)
    gcol = jnp.where(lane < H, gb, _log_sigmoid(gb))
    grow = gcol.T[:2 * H]
    r_i = lax.broadcasted_iota(jnp.int32, (L, L), 0)
    c_i = lax.broadcasted_iota(jnp.int32, (L, L), 1)
    causal = c_i <= r_i
    tri_lo = jnp.where(causal, 1.0, 0.0).astype(BF16)
    tri_up = jnp.where(c_i >= r_i, 1.0, 0.0).astype(BF16)
    bcol = sum(jnp.dot(tri_lo, p, preferred_element_type=F32) for p in _split_bf16(gcol, 3))
    brow = sum(jnp.dot(p, tri_up, preferred_element_type=F32) for p in _split_bf16(grow, 3))

    for h in range(H):
        li_row = grow[h:h + 1, :]
        li_col = gcol[:, h:h + 1]
        b_row = brow[H + h:H + h + 1, :]
        b_col = bcol[:, H + h:H + h + 1]
        m_prev = m_ref[h][0:1, 0:1]
        q = qk[:, h * DK:(h + 1) * DK] * (DK ** -0.5)
        k = qk[:, (H + h) * DK:(H + h + 1) * DK]
        vh = v_ref[:, h * DV:(h + 1) * DV]
        qb = q.astype(BF16)

        dmat = jnp.where(causal, b_col - b_row + li_row, -jnp.inf)
        inter = b_col + m_prev
        m_t = jnp.maximum(inter, jnp.max(dmat, axis=1, keepdims=True))
        w_intra = jnp.exp(dmat - m_t)
        s = lax.dot_general(qb, k.astype(BF16), (((1,), (1,)), ((), ())),
                            preferred_element_type=F32) * w_intra
        w_inter = jnp.exp(inter - m_t)
        num = (jnp.dot(s.astype(BF16), vh, preferred_element_type=F32)
               + w_inter * jnp.dot(qb, C_ref[h].astype(BF16), preferred_element_type=F32))
        den = (jnp.sum(s, axis=1, keepdims=True)
               + w_inter * jnp.sum(q * n_ref[h], axis=1, keepdims=True))
        den = jnp.maximum(jnp.abs(den), jnp.exp(-m_t))
        hh = num / den

        b_last = b_col[L - 1:L, :]
        wdec_row = b_last - b_row + li_row
        wdec_col = b_last - b_col + li_col
        m_new = jnp.maximum(b_last + m_prev, jnp.max(wdec_row, axis=1, keepdims=True))
        kw = k * jnp.exp(wdec_col - m_new)
        decay = jnp.exp(b_last + m_prev - m_new)
        C_ref[h] = decay * C_ref[h] + lax.dot_general(
            kw.astype(BF16), vh, (((0,), (0,)), ((), ())), preferred_element_type=F32)
        n_ref[h] = decay * n_ref[h] + jnp.sum(kw, axis=0, keepdims=True)
        m_ref[h] = jnp.broadcast_to(m_new, m_ref.shape[1:])

        og = _sigmoid(o_ref[:, h * DV:(h + 1) * DV].astype(F32)) * hh
        mu = jnp.mean(og, axis=-1, keepdims=True)
        d = og - mu
        var = jnp.mean(d * d, axis=-1, keepdims=True)
        hn = d * lax.rsqrt(var + LN_EPS) * hn_ref[:, h * DV:(h + 1) * DV]
        out_ref[:, h * DV:(h + 1) * DV] = (
            hn * _silu(z_ref[:, h * DV:(h + 1) * DV].astype(F32))).astype(BF16)


def _mlstm(p, gates, conv_w, gate_bias, head_norm_w, *, L):
    S = p.shape[0]
    L = min(L, S)
    QK2 = 2 * A_QK
    blk = lambda col: pl.BlockSpec((L, A_WIDTH), lambda c: (c, col))
    return pl.pallas_call(
        functools.partial(_mlstm_kernel, L=L),
        grid=(S // L,),
        in_specs=[pl.BlockSpec((L, QK2), lambda c: (c, 0)),
                  pl.BlockSpec((HALO, QK2), lambda c: (jnp.maximum(c * (L // HALO) - 1, 0), 0)),
                  blk(1), blk(2), blk(3),
                  pl.BlockSpec((L, GATE_PAD), lambda c: (c, 0)),
                  pl.BlockSpec((CONV_WIDTH, QK2), lambda c: (0, 0)),
                  pl.BlockSpec((1, GATE_PAD), lambda c: (0, 0)),
                  pl.BlockSpec((1, A_WIDTH), lambda c: (0, 0))],
        out_specs=pl.BlockSpec((L, A_WIDTH), lambda c: (c, 0)),
        out_shape=jax.ShapeDtypeStruct((S, A_WIDTH), BF16),
        scratch_shapes=[pltpu.VMEM((A_HEADS, A_QK_DIM, A_V_DIM), F32),
                        pltpu.VMEM((A_HEADS, 1, A_QK_DIM), F32),
                        pltpu.VMEM((A_HEADS, 8, LANE), F32)],
        compiler_params=_cparams(("arbitrary",)),
        name="mlstm",
    )(p, p, p, p, p, gates, conv_w, gate_bias, head_norm_w.reshape(1, A_WIDTH))


ATTN_HEADS_PER_STEP = 2
ATTN_TQ = 512
ATTN_TK = 128


def _attn_kernel(q_ref, k_ref, v_ref, z_ref, o_ref, acc_ref, carry_ref, score_ref, p_ref):
    i = pl.program_id(1)
    dh, NH, TQ, TK = B_HEAD_DIM, ATTN_HEADS_PER_STEP, ATTN_TQ, ATTN_TK
    TH = TQ // 2
    NSUB = TH // TK
    s_i = lax.broadcasted_iota(jnp.int32, (TK, TH), 0)
    t_i = lax.broadcasted_iota(jnp.int32, (TK, TH), 1)
    r_i = lax.broadcasted_iota(jnp.int32, (TK, TK), 0)
    j_i = lax.broadcasted_iota(jnp.int32, (TK, TK), 1)
    tri = jnp.where(j_i >= r_i, 1.0, 0.0).astype(BF16)
    tri2 = jnp.concatenate([tri, tri], axis=1)
    q_t = [q_ref[:, n * dh:(n + 1) * dh].astype(F32).T.astype(BF16) for n in range(NH)]

    acc_ref[...] = jnp.zeros_like(acc_ref)
    carry_ref[...] = jnp.zeros_like(carry_ref)

    def softplus_parts(z, mask):
        sp = jnp.where(z > SOFTPLUS_LINEAR_ABOVE, z, jnp.log(1.0 + jnp.exp(z)))
        if mask is not None:
            sp = jnp.where(mask, sp, 0.0)
        return jnp.concatenate(_split_bf16(sp, 2), axis=0)

    def diag_block(start, hf, own):
        chains = [(n, b) for n in range(NH) for b in reversed(range(NSUB))]
        mask = {b: (s_i + b * TK) < t_i if own else None for b in range(NSUB)}
        zs, parts, sums = {}, {}, {}
        for (n, b) in chains:
            kb = k_ref[pl.ds(start + b * TK, TK), n * dh:(n + 1) * dh]
            zs[n, b] = jnp.dot(kb, q_t[n][:, hf * TH:(hf + 1) * TH], preferred_element_type=F32)
        for (n, b) in chains:
            parts[n, b] = softplus_parts(zs[n, b], mask[b])
        for c in chains:
            sums[c] = jnp.dot(tri2, parts[c], preferred_element_type=F32)
        for n in range(NH):
            carry = carry_ref[n, hf]
            ps = [None] * NSUB
            for b in reversed(range(NSUB)):
                p = jnp.exp(zs[n, b] - sums[n, b] - carry)
                if own:
                    p = jnp.where(mask[b], p, 0.0)
                carry = carry + sums[n, b][0:1, :]
                ps[b] = p.astype(BF16)
            carry_ref[n, hf] = carry
            vb = v_ref[pl.ds(start, TH), n * dh:(n + 1) * dh]
            acc_ref[n, hf] += lax.dot_general(vb, jnp.concatenate(ps, axis=0), (((0,), (0,)), ((), ())),
                                              preferred_element_type=F32)

    def scores(start):
        for n in range(NH):
            for b in range(NSUB):
                kb = k_ref[pl.ds(start + b * TK, TK), n * dh:(n + 1) * dh]
                score_ref[n, b] = jnp.dot(kb, q_t[n], preferred_element_type=F32)

    def weighted_values(start):
        for n in range(NH):
            vb = v_ref[pl.ds(start, TH), n * dh:(n + 1) * dh]
            for hf in range(2):
                acc_ref[n, hf] += lax.dot_general(vb, p_ref[n, hf], (((0,), (0,)), ((), ())),
                                                  preferred_element_type=F32)

    def body(jj, _):
        chains = [(n, hf, b) for n in range(NH) for hf in range(2) for b in reversed(range(NSUB))]
        start = pl.multiple_of(base - (jj + 1) * TH, TH)
        zs, parts, sums = {}, {}, {}
        for (n, hf, b) in chains:
            zs[n, hf, b] = score_ref[n, b, :, hf * TH:(hf + 1) * TH]
        scores(pl.multiple_of(jnp.maximum(start - TH, 0), TH))
        weighted_values(pl.multiple_of(start + TH, TH))
        for c in chains:
            parts[c] = softplus_parts(zs[c], None)
        for c in chains:
            sums[c] = jnp.dot(tri2, parts[c], preferred_element_type=F32)
        for n in range(NH):
            for hf in range(2):
                carry = carry_ref[n, hf]
                for b in reversed(range(NSUB)):
                    p = jnp.exp(zs[n, hf, b] - sums[n, hf, b] - carry)
                    carry = carry + sums[n, hf, b][0:1, :]
                    p_ref[n, hf, b * TK:(b + 1) * TK, :] = p.astype(BF16)
                carry_ref[n, hf] = carry
        return 0

    base = pl.multiple_of(i * TQ, TQ)
    diag_block(base + TH, 1, True)
    diag_block(base, 1, False)
    diag_block(base, 0, True)
    scores(pl.multiple_of(jnp.maximum(base - TH, 0), TH))
    p_ref[...] = jnp.zeros_like(p_ref)
    lax.fori_loop(0, 2 * i, body, 0)
    weighted_values(0)
    for n in range(NH):
        for hf in range(2):
            gate = _silu(z_ref[hf * TH:(hf + 1) * TH, n * dh:(n + 1) * dh].astype(F32))
            o_ref[hf * TH:(hf + 1) * TH, n * dh:(n + 1) * dh] = (acc_ref[n, hf].T * gate).astype(BF16)


def _stick_breaking(p):
    S = p.shape[0]
    NH, TQ, dh = ATTN_HEADS_PER_STEP, ATTN_TQ, B_HEAD_DIM
    assert S % TQ == 0
    W = NH * dh
    c0 = (2 * A_QK + 3 * A_WIDTH) // W
    nb = B_WIDTH // W
    return pl.pallas_call(
        _attn_kernel,
        grid=(nb, S // TQ),
        in_specs=[pl.BlockSpec((TQ, W), lambda h, i: (i, c0 + h)),
                  pl.BlockSpec((S, W), lambda h, i: (0, c0 + nb + h)),
                  pl.BlockSpec((S, W), lambda h, i: (0, c0 + 2 * nb + h)),
                  pl.BlockSpec((TQ, W), lambda h, i: (i, c0 + 3 * nb + h))],
        out_specs=pl.BlockSpec((TQ, W), lambda h, i: (i, h)),
        out_shape=jax.ShapeDtypeStruct((S, B_WIDTH), BF16),
        scratch_shapes=[pltpu.VMEM((NH, 2, dh, TQ // 2), F32),
                        pltpu.VMEM((NH, 2, 1, TQ // 2), F32),
                        pltpu.VMEM((NH, (TQ // 2) // ATTN_TK, ATTN_TK, TQ), F32),
                        pltpu.VMEM((NH, 2, TQ // 2, TQ // 2), BF16)],
        compiler_params=_cparams(("arbitrary", "arbitrary")),
        name="stick_breaking",
    )(p, p, p, p)


def _pool_kernel(v_ref, halo_ref, z_ref, w_ref, b_ref, s_ref, o_ref, *, tm):
    i = pl.program_id(0)
    GD = C_GROUP_DIM
    pos = (lax.broadcasted_iota(jnp.int32, (tm, 1), 0) + i * tm + 1).astype(F32)
    for g, win in enumerate(POOL_WINDOWS):
        cur = v_ref[:, g * GD:(g + 1) * GD].astype(F32)
        halo = jnp.where(i > 0, halo_ref[:, g * GD:(g + 1) * GD].astype(F32), 0.0)
        a = jnp.concatenate([halo, cur], axis=0)
        step = 1
        while step < win:
            a = a + pltpu.roll(a, step, axis=0)
            step *= 2
        pooled = a[HALO:] / jnp.minimum(pos, float(win)) - cur
        mixed = jnp.dot(pooled.astype(BF16), w_ref[g], preferred_element_type=F32) + b_ref[g]
        gate = _silu(z_ref[:, g * GD:(g + 1) * GD].astype(F32))
        o_ref[:, g * GD:(g + 1) * GD] = (mixed * s_ref[:, g * GD:(g + 1) * GD] * gate).astype(BF16)


def _pool_mixer(p, pool_w, pool_b, pool_scale, *, tm):
    S = p.shape[0]
    tm = min(tm, S)
    W = C_GROUPS * C_GROUP_DIM
    return pl.pallas_call(
        functools.partial(_pool_kernel, tm=tm),
        grid=(S // tm,),
        in_specs=[pl.BlockSpec((tm, W), lambda i: (i, 0)),
                  pl.BlockSpec((HALO, W), lambda i: (jnp.maximum(i * (tm // HALO) - 1, 0), 0)),
                  pl.BlockSpec((tm, W), lambda i: (i, 1)),
                  pl.BlockSpec((C_GROUPS, C_GROUP_DIM, C_GROUP_DIM), lambda i: (0, 0, 0)),
                  pl.BlockSpec((C_GROUPS, 1, C_GROUP_DIM), lambda i: (0, 0, 0)),
                  pl.BlockSpec((1, W), lambda i: (0, 0))],
        out_specs=pl.BlockSpec((tm, W), lambda i: (i, 0)),
        out_shape=jax.ShapeDtypeStruct((S, W), BF16),
        compiler_params=_cparams(("arbitrary",)),
        name="pool_mixer",
    )(p, p, p, pool_w.astype(BF16), pool_b.reshape(C_GROUPS, 1, C_GROUP_DIM), pool_scale.reshape(1, W))


def _even_layer(u, w_in, conv_w, igate_b, fgate_b, head_norm_w, w_out):
    D = w_in.shape[0]
    w_main = jnp.concatenate([w_in[:, :GATE_COL0], w_in[:, GATE_COL0 + 2 * A_HEADS:]], axis=1).astype(BF16)
    w_gate = jnp.pad(w_in[:, GATE_COL0:GATE_COL0 + 2 * A_HEADS],
                     ((0, 0), (0, GATE_PAD - 2 * A_HEADS))).astype(BF16)
    n_main = w_main.shape[1]
    col = jnp.arange(n_main)
    col_scale = jnp.where((col >= GATE_COL0) & (col < GATE_COL0 + B_WIDTH),
                          B_HEAD_DIM ** -0.5, 1.0).astype(F32).reshape(1, n_main)
    p = _matmul(u, w_main, col_scale, BF16, tm=1024, tn=1024, name="in_proj_even")
    gates = _matmul(u, w_gate, jnp.ones((1, GATE_PAD), F32), F32, tm=1024, tn=GATE_PAD, name="gate_proj")
    gate_bias = jnp.pad(jnp.concatenate([igate_b, fgate_b]), (0, GATE_PAD - 2 * A_HEADS)).reshape(1, GATE_PAD)
    mix_a = _mlstm(p, gates, conv_w, gate_bias, head_norm_w, L=256)
    mix_b = _stick_breaking(p)
    w_out = w_out.astype(BF16)
    return _matmul2(mix_a, w_out[:A_WIDTH], mix_b, w_out[A_WIDTH:], tm=1024, tn=1024, name="out_proj_even")


def _odd_layer(u, w_in, pool_w, pool_b, pool_scale, w_out):
    D = w_in.shape[0]
    n_in = w_in.shape[1]
    p = _matmul(u, w_in.astype(BF16), jnp.ones((1, n_in), F32), BF16, tm=1024, tn=1024, name="in_proj_odd")
    h = _pool_mixer(p, pool_w, pool_b, pool_scale, tm=256)
    return _matmul(h, w_out.astype(BF16), jnp.ones((1, D), F32), F32, tm=1024, tn=1024, name="out_proj_odd")


def kernel(x, c, ada_w_0, ada_b_0, w_in_0, conv_w_0, igate_b_0, fgate_b_0, head_norm_w_0, w_out_0, ln_g_0, ln_b_0, ada_w_1, ada_b_1, w_in_1, pool_w_1, pool_b_1, pool_scale_1, w_out_1, ln_g_1, ln_b_1, ada_w_2, ada_b_2, w_in_2, conv_w_2, igate_b_2, fgate_b_2, head_norm_w_2, w_out_2, ln_g_2, ln_b_2, ada_w_3, ada_b_3, w_in_3, pool_w_3, pool_b_3, pool_scale_3, w_out_3, ln_g_3, ln_b_3):
    B, S, D = x.shape
    assert B == 1
    ada = [(ada_w_0, ada_b_0), (ada_w_1, ada_b_1), (ada_w_2, ada_b_2), (ada_w_3, ada_b_3)]
    post = [(ln_g_0, ln_b_0), (ln_g_1, ln_b_1), (ln_g_2, ln_b_2), (ln_g_3, ln_b_3)]
    mixers = [
        (w_in_0, conv_w_0, igate_b_0, fgate_b_0, head_norm_w_0, w_out_0),
        (w_in_1, pool_w_1, pool_b_1, pool_scale_1, w_out_1),
        (w_in_2, conv_w_2, igate_b_2, fgate_b_2, head_norm_w_2, w_out_2),
        (w_in_3, pool_w_3, pool_b_3, pool_scale_3, w_out_3),
    ]
    c_col = c.reshape(D, 1)
    mods = [_modulation(c_col, w, b) for (w, b) in ada]
    xs = x.reshape(S, D)
    u = _modulate(xs, mods[0])
    for l in range(DEPTH):
        y = _even_layer(u, *mixers[l]) if l % 2 == 0 else _odd_layer(u, *mixers[l])
        xs, u = _residual_ln(xs, y, mods[l], *post[l], mods[l + 1] if l + 1 < DEPTH else None)
    return xs.reshape(B, S, D)
```

```python
import functools

import jax
import jax.numpy as jnp
from jax import lax
from jax.experimental import pallas as pl
from jax.experimental.pallas import tpu as pltpu

F32 = jnp.float32
BF16 = jnp.bfloat16

DEPTH = 4
A_HEADS = 4
A_QK_DIM = 256
A_V_DIM = 512
A_QK = A_HEADS * A_QK_DIM
A_WIDTH = A_HEADS * A_V_DIM
CONV_WIDTH = 4
B_HEADS = 8
B_HEAD_DIM = 128
B_WIDTH = B_HEADS * B_HEAD_DIM
POOL_WINDOWS = (2, 4, 8, 16)
C_GROUPS = 4
C_GROUP_DIM = 1024
GATE_COL0 = 2 * A_QK + 3 * A_WIDTH
GATE_PAD = 128
DEEPNORM_ALPHA = (2 * DEPTH) ** 0.25
LN_EPS = 1e-5
SOFTPLUS_LINEAR_ABOVE = 20.0

LANE = 128
HALO = 16
VMEM_LIMIT = 56 * 1024 * 1024


def _cparams(sem):
    return pltpu.CompilerParams(dimension_semantics=sem, vmem_limit_bytes=VMEM_LIMIT)


def _sigmoid(x):
    return 1.0 / (1.0 + jnp.exp(-x))


def _silu(x):
    return x * _sigmoid(x)


def _log_sigmoid(x):
    return jnp.minimum(x, 0.0) - jnp.log1p(jnp.exp(-jnp.abs(x)))


def _split_bf16(x, parts):
    out = []
    r = x
    for _ in range(parts - 1):
        h = r.astype(BF16)
        out.append(h)
        r = r - h.astype(F32)
    out.append(r.astype(BF16))
    return out


def _mod_kernel(c_ref, w_ref, b_ref, o_ref):
    c = c_ref[...]
    o_ref[...] = jnp.sum(_silu(c) * w_ref[...], axis=0, keepdims=True) + b_ref[...]


def _modulation(c_col, w, b):
    D, N = w.shape
    tn = 512
    return pl.pallas_call(
        _mod_kernel,
        grid=(N // tn,),
        in_specs=[pl.BlockSpec((D, 1), lambda j: (0, 0)),
                  pl.BlockSpec((D, tn), lambda j: (0, j)),
                  pl.BlockSpec((1, tn), lambda j: (0, j))],
        out_specs=pl.BlockSpec((1, tn), lambda j: (0, j)),
        out_shape=jax.ShapeDtypeStruct((1, N), F32),
        compiler_params=_cparams(("arbitrary",)),
        name="modulation",
    )(c_col, w, b.reshape(1, N))


def _modulate_kernel(x_ref, sh_ref, sc_ref, u_ref):
    u_ref[...] = (x_ref[...] * (1.0 + sc_ref[...]) + sh_ref[...]).astype(BF16)


def _modulate(x, mod):
    S, D = x.shape
    tm = min(512, S)
    return pl.pallas_call(
        _modulate_kernel,
        grid=(S // tm,),
        in_specs=[pl.BlockSpec((tm, D), lambda i: (i, 0)),
                  pl.BlockSpec((1, D), lambda i: (0, 0)),
                  pl.BlockSpec((1, D), lambda i: (0, 1))],
        out_specs=pl.BlockSpec((tm, D), lambda i: (i, 0)),
        out_shape=jax.ShapeDtypeStruct((S, D), BF16),
        compiler_params=_cparams(("arbitrary",)),
        name="modulate",
    )(x, mod, mod)


def _mm_kernel(a_ref, b_ref, o_ref):
    o_ref[...] = jnp.dot(a_ref[...], b_ref[...], preferred_element_type=F32).astype(o_ref.dtype)


def _mm_scaled_kernel(a_ref, b_ref, s_ref, o_ref):
    acc = jnp.dot(a_ref[...], b_ref[...], preferred_element_type=F32)
    o_ref[...] = (acc * s_ref[...]).astype(o_ref.dtype)


def _matmul(a, b, out_dtype, *, tm, tn, name, n_cols=None, col_scale=None):
    M, K = a.shape
    N = b.shape[1] if n_cols is None else n_cols
    tm, tn = min(tm, M), min(tn, N)
    assert M % tm == 0 and N % tn == 0
    in_specs = [pl.BlockSpec((tm, K), lambda i, j: (i, 0)),
                pl.BlockSpec((K, tn), lambda i, j: (0, j))]
    args = [a, b]
    if col_scale is not None:
        in_specs.append(pl.BlockSpec((1, tn), lambda i, j: (0, j)))
        args.append(col_scale)
    return pl.pallas_call(
        _mm_kernel if col_scale is None else _mm_scaled_kernel,
        grid=(M // tm, N // tn),
        in_specs=in_specs,
        out_specs=pl.BlockSpec((tm, tn), lambda i, j: (i, j)),
        out_shape=jax.ShapeDtypeStruct((M, N), out_dtype),
        compiler_params=_cparams(("arbitrary", "arbitrary")),
        name=name,
    )(*args)


def _mm2_kernel(a1_ref, b1_ref, a2_ref, b2_ref, o_ref):
    o_ref[...] = (jnp.dot(a1_ref[...], b1_ref[...], preferred_element_type=F32)
                  + jnp.dot(a2_ref[...], b2_ref[...], preferred_element_type=F32))


def _matmul2(a1, a2, b, *, tm, tn, name):
    M, K1 = a1.shape
    K2 = a2.shape[1]
    N = b.shape[1]
    assert b.shape[0] == K1 + K2 and K1 % K2 == 0
    tm, tn = min(tm, M), min(tn, N)
    return pl.pallas_call(
        _mm2_kernel,
        grid=(M // tm, N // tn),
        in_specs=[pl.BlockSpec((tm, K1), lambda i, j: (i, 0)),
                  pl.BlockSpec((K1, tn), lambda i, j: (0, j)),
                  pl.BlockSpec((tm, K2), lambda i, j: (i, 0)),
                  pl.BlockSpec((K2, tn), lambda i, j: (K1 // K2, j))],
        out_specs=pl.BlockSpec((tm, tn), lambda i, j: (i, j)),
        out_shape=jax.ShapeDtypeStruct((M, N), F32),
        compiler_params=_cparams(("arbitrary", "arbitrary")),
        name=name,
    )(a1, b, a2, b)


def _ln_kernel(x_ref, y_ref, gate_ref, g_ref, b_ref, *rest, with_next):
    v = DEEPNORM_ALPHA * x_ref[...] + (1.0 + gate_ref[...]) * y_ref[...]
    mu = jnp.mean(v, axis=-1, keepdims=True)
    d = v - mu
    var = jnp.mean(d * d, axis=-1, keepdims=True)
    xn = d * lax.rsqrt(var + LN_EPS) * g_ref[...] + b_ref[...]
    if with_next:
        sh_ref, sc_ref, xo_ref, u_ref = rest
        u_ref[...] = (xn * (1.0 + sc_ref[...]) + sh_ref[...]).astype(BF16)
    else:
        (xo_ref,) = rest
    xo_ref[...] = xn


def _residual_ln(x, y, mod, ln_g, ln_b, mod_next):
    S, D = x.shape
    tm = min(256, S)
    row = pl.BlockSpec((tm, D), lambda i: (i, 0))

    def vec(k):
        return pl.BlockSpec((1, D), lambda i: (0, k))

    in_specs = [row, row, vec(2), vec(0), vec(0)]
    args = [x, y, mod, ln_g.reshape(1, D), ln_b.reshape(1, D)]
    out_specs = [row]
    out_shape = [jax.ShapeDtypeStruct((S, D), F32)]
    if mod_next is not None:
        in_specs += [vec(0), vec(1)]
        args += [mod_next, mod_next]
        out_specs.append(row)
        out_shape.append(jax.ShapeDtypeStruct((S, D), BF16))
    res = pl.pallas_call(
        functools.partial(_ln_kernel, with_next=mod_next is not None),
        grid=(S // tm,),
        in_specs=in_specs,
        out_specs=out_specs,
        out_shape=out_shape,
        compiler_params=_cparams(("arbitrary",)),
        name="residual_ln",
    )(*args)
    return (res[0], res[1]) if mod_next is not None else (res[0], None)


def _mlstm_kernel(qk_ref, halo_ref, v_ref, o_ref, z_ref, g_ref, conv_ref, gb_ref, hn_ref,
                  out_ref, C_ref, n_ref, m_ref, *, L):
    H, DK, DV = A_HEADS, A_QK_DIM, A_V_DIM
    c = pl.program_id(0)

    @pl.when(c == 0)
    def _():
        C_ref[...] = jnp.zeros_like(C_ref)
        n_ref[...] = jnp.zeros_like(n_ref)
        m_ref[...] = jnp.zeros_like(m_ref)

    cur = qk_ref[...].astype(F32)
    halo = jnp.where(c > 0, halo_ref[...].astype(F32), 0.0)
    xp = jnp.concatenate([halo, cur], axis=0)
    cw = conv_ref[...]
    acc = cw[CONV_WIDTH - 1:CONV_WIDTH] * cur
    for j in range(CONV_WIDTH - 1):
        shifted = pltpu.roll(xp, CONV_WIDTH - 1 - j, axis=0)[HALO:]
        acc = acc + cw[j:j + 1] * shifted
    qk = _silu(acc)

    gb = g_ref[...] + gb_ref[...]
    lane = lax.broadcasted_iota(jnp.int32, gb.shape, 1)
    gcol = jnp.where(lane < H, gb, _log_sigmoid(gb))
    grow = gcol.T[:2 * H]
    r_i = lax.broadcasted_iota(jnp.int32, (L, L), 0)
    c_i = lax.broadcasted_iota(jnp.int32, (L, L), 1)
    causal = c_i <= r_i
    tri_lo = jnp.where(causal, 1.0, 0.0).astype(BF16)
    tri_up = jnp.where(c_i >= r_i, 1.0, 0.0).astype(BF16)
    bcol = sum(jnp.dot(tri_lo, p, preferred_element_type=F32) for p in _split_bf16(gcol, 3))
    brow = sum(jnp.dot(p, tri_up, preferred_element_type=F32) for p in _split_bf16(grow, 3))

    for h in range(H):
        li_row = grow[h:h + 1, :]
        li_col = gcol[:, h:h + 1]
        b_row = brow[H + h:H + h + 1, :]
        b_col = bcol[:, H + h:H + h + 1]
        m_prev = m_ref[h][0:1, 0:1]
        q = qk[:, h * DK:(h + 1) * DK] * (DK ** -0.5)
        k = qk[:, (H + h) * DK:(H + h + 1) * DK]
        vh = v_ref[:, h * DV:(h + 1) * DV]
        qb = q.astype(BF16)

        dmat = jnp.where(causal, b_col - b_row + li_row, -jnp.inf)
        inter = b_col + m_prev
        m_t = jnp.maximum(inter, jnp.max(dmat, axis=1, keepdims=True))
        w_intra = jnp.exp(dmat - m_t)
        s = lax.dot_general(qb, k.astype(BF16), (((1,), (1,)), ((), ())),
                            preferred_element_type=F32) * w_intra
        w_inter = jnp.exp(inter - m_t)
        num = (jnp.dot(s.astype(BF16), vh, preferred_element_type=F32)
               + w_inter * jnp.dot(qb, C_ref[h].astype(BF16), preferred_element_type=F32))
        den = (jnp.sum(s, axis=1, keepdims=True)
               + w_inter * jnp.sum(q * n_ref[h], axis=1, keepdims=True))
        den = jnp.maximum(jnp.abs(den), jnp.exp(-m_t))
        hh = num / den

        b_last = b_col[L - 1:L, :]
        wdec_row = b_last - b_row + li_row
        wdec_col = b_last - b_col + li_col
        m_new = jnp.maximum(b_last + m_prev, jnp.max(wdec_row, axis=1, keepdims=True))
        kw = k * jnp.exp(wdec_col - m_new)
        decay = jnp.exp(b_last + m_prev - m_new)
        C_ref[h] = decay * C_ref[h] + lax.dot_general(
            kw.astype(BF16), vh, (((0,), (0,)), ((), ())), preferred_element_type=F32)
        n_ref[h] = decay * n_ref[h] + jnp.sum(kw, axis=0, keepdims=True)
        m_ref[h] = jnp.broadcast_to(m_new, m_ref.shape[1:])

        og = _sigmoid(o_ref[:, h * DV:(h + 1) * DV].astype(F32)) * hh
        mu = jnp.mean(og, axis=-1, keepdims=True)
        d = og - mu
        var = jnp.mean(d * d, axis=-1, keepdims=True)
        hn = d * lax.rsqrt(var + LN_EPS) * hn_ref[:, h * DV:(h + 1) * DV]
        out_ref[:, h * DV:(h + 1) * DV] = (
            hn * _silu(z_ref[:, h * DV:(h + 1) * DV].astype(F32))).astype(BF16)


def _mlstm(p, gates, conv_w, gate_bias, head_norm_w, *, L):
    S = p.shape[0]
    L = min(L, S)
    QK2 = 2 * A_QK
    blk = lambda col: pl.BlockSpec((L, A_WIDTH), lambda c: (c, col))
    return pl.pallas_call(
        functools.partial(_mlstm_kernel, L=L),
        grid=(S // L,),
        in_specs=[pl.BlockSpec((L, QK2), lambda c: (c, 0)),
                  pl.BlockSpec((HALO, QK2), lambda c: (jnp.maximum(c * (L // HALO) - 1, 0), 0)),
                  blk(1), blk(2), blk(3),
                  pl.BlockSpec((L, GATE_PAD), lambda c: (c, 0)),
                  pl.BlockSpec((CONV_WIDTH, QK2), lambda c: (0, 0)),
                  pl.BlockSpec((1, GATE_PAD), lambda c: (0, 0)),
                  pl.BlockSpec((1, A_WIDTH), lambda c: (0, 0))],
        out_specs=pl.BlockSpec((L, A_WIDTH), lambda c: (c, 0)),
        out_shape=jax.ShapeDtypeStruct((S, A_WIDTH), BF16),
        scratch_shapes=[pltpu.VMEM((A_HEADS, A_QK_DIM, A_V_DIM), F32),
                        pltpu.VMEM((A_HEADS, 1, A_QK_DIM), F32),
                        pltpu.VMEM((A_HEADS, 8, LANE), F32)],
        compiler_params=_cparams(("arbitrary",)),
        name="mlstm",
    )(p, p, p, p, p, gates, conv_w, gate_bias, head_norm_w.reshape(1, A_WIDTH))


ATTN_HEADS_PER_STEP = 2
ATTN_TQ = 512
ATTN_TK = 128


def _attn_kernel(q_ref, k_ref, v_ref, z_ref, o_ref, acc_ref, carry_ref, score_ref, p_ref):
    i = pl.program_id(1)
    dh, NH, TQ, TK = B_HEAD_DIM, ATTN_HEADS_PER_STEP, ATTN_TQ, ATTN_TK
    TH = TQ // 2
    NSUB = TH // TK
    s_i = lax.broadcasted_iota(jnp.int32, (TK, TH), 0)
    t_i = lax.broadcasted_iota(jnp.int32, (TK, TH), 1)
    r_i = lax.broadcasted_iota(jnp.int32, (TK, TK), 0)
    j_i = lax.broadcasted_iota(jnp.int32, (TK, TK), 1)
    tri = jnp.where(j_i >= r_i, 1.0, 0.0).astype(BF16)
    q_t = [q_ref[:, n * dh:(n + 1) * dh].astype(F32).T.astype(BF16) for n in range(NH)]

    acc_ref[...] = jnp.zeros_like(acc_ref)
    carry_ref[...] = jnp.zeros_like(carry_ref)

    def softplus_parts(z, mask):
        sp = jnp.where(z > SOFTPLUS_LINEAR_ABOVE, z, jnp.log(1.0 + jnp.exp(z)))
        if mask is not None:
            sp = jnp.where(mask, sp, 0.0)
        return sp.astype(BF16)

    def diag_block(start, hf, own):
        chains = [(n, b) for n in range(NH) for b in reversed(range(NSUB))]
        mask = {b: (s_i + b * TK) < t_i if own else None for b in range(NSUB)}
        zs, parts, sums = {}, {}, {}
        for (n, b) in chains:
            kb = k_ref[pl.ds(start + b * TK, TK), n * dh:(n + 1) * dh]
            zs[n, b] = jnp.dot(kb, q_t[n][:, hf * TH:(hf + 1) * TH], preferred_element_type=F32)
        for (n, b) in chains:
            parts[n, b] = softplus_parts(zs[n, b], mask[b])
        for c in chains:
            sums[c] = jnp.dot(tri, parts[c], preferred_element_type=F32)
        for n in range(NH):
            carry = carry_ref[n, hf]
            ps = [None] * NSUB
            for b in reversed(range(NSUB)):
                p = jnp.exp(zs[n, b] - sums[n, b] - carry)
                if own:
                    p = jnp.where(mask[b], p, 0.0)
                carry = carry + sums[n, b][0:1, :]
                ps[b] = p.astype(BF16)
            carry_ref[n, hf] = carry
            vb = v_ref[pl.ds(start, TH), n * dh:(n + 1) * dh]
            acc_ref[n, hf] += lax.dot_general(vb, jnp.concatenate(ps, axis=0), (((0,), (0,)), ((), ())),
                                              preferred_element_type=F32)

    def scores(start):
        for n in range(NH):
            for b in range(NSUB):
                kb = k_ref[pl.ds(start + b * TK, TK), n * dh:(n + 1) * dh]
                score_ref[n, b] = jnp.dot(kb, q_t[n], preferred_element_type=F32)

    def weighted_values(start):
        for n in range(NH):
            vb = v_ref[pl.ds(start, TH), n * dh:(n + 1) * dh]
            for hf in range(2):
                acc_ref[n, hf] += lax.dot_general(vb, p_ref[n, hf], (((0,), (0,)), ((), ())),
                                                  preferred_element_type=F32)

    def body(jj, _):
        chains = [(n, hf, b) for n in range(NH) for hf in range(2) for b in reversed(range(NSUB))]
        start = pl.multiple_of(base - (jj + 1) * TH, TH)
        zs, parts, sums = {}, {}, {}
        for (n, hf, b) in chains:
            zs[n, hf, b] = score_ref[n, b, :, hf * TH:(hf + 1) * TH]
        scores(pl.multiple_of(jnp.maximum(start - TH, 0), TH))
        weighted_values(pl.multiple_of(start + TH, TH))
        for c in chains:
            parts[c] = softplus_parts(zs[c], None)
        for c in chains:
            sums[c] = jnp.dot(tri, parts[c], preferred_element_type=F32)
        for n in range(NH):
            for hf in range(2):
                carry = carry_ref[n, hf]
                for b in reversed(range(NSUB)):
                    p = jnp.exp(zs[n, hf, b] - sums[n, hf, b] - carry)
                    carry = carry + sums[n, hf, b][0:1, :]
                    p_ref[n, hf, b * TK:(b + 1) * TK, :] = p.astype(BF16)
                carry_ref[n, hf] = carry
        return 0

    base = pl.multiple_of(i * TQ, TQ)
    diag_block(base + TH, 1, True)
    diag_block(base, 1, False)
    diag_block(base, 0, True)
    scores(pl.multiple_of(jnp.maximum(base - TH, 0), TH))
    p_ref[...] = jnp.zeros_like(p_ref)
    lax.fori_loop(0, 2 * i, body, 0)
    weighted_values(0)
    for n in range(NH):
        for hf in range(2):
            gate = _silu(z_ref[hf * TH:(hf + 1) * TH, n * dh:(n + 1) * dh].astype(F32))
            o_ref[hf * TH:(hf + 1) * TH, n * dh:(n + 1) * dh] = (acc_ref[n, hf].T * gate).astype(BF16)


def _stick_breaking(p):
    S = p.shape[0]
    NH, TQ, dh = ATTN_HEADS_PER_STEP, ATTN_TQ, B_HEAD_DIM
    assert S % TQ == 0
    W = NH * dh
    nb = B_WIDTH // W
    return pl.pallas_call(
        _attn_kernel,
        grid=(nb, S // TQ),
        in_specs=[pl.BlockSpec((TQ, W), lambda h, i: (i, h)),
                  pl.BlockSpec((S, W), lambda h, i: (0, nb + h)),
                  pl.BlockSpec((S, W), lambda h, i: (0, 2 * nb + h)),
                  pl.BlockSpec((TQ, W), lambda h, i: (i, 3 * nb + h))],
        out_specs=pl.BlockSpec((TQ, W), lambda h, i: (i, h)),
        out_shape=jax.ShapeDtypeStruct((S, B_WIDTH), BF16),
        scratch_shapes=[pltpu.VMEM((NH, 2, dh, TQ // 2), F32),
                        pltpu.VMEM((NH, 2, 1, TQ // 2), F32),
                        pltpu.VMEM((NH, (TQ // 2) // ATTN_TK, ATTN_TK, TQ), F32),
                        pltpu.VMEM((NH, 2, TQ // 2, TQ // 2), BF16)],
        compiler_params=_cparams(("arbitrary", "arbitrary")),
        name="stick_breaking",
    )(p, p, p, p)


def _pool_kernel(v_ref, halo_ref, z_ref, w_ref, b_ref, s_ref, o_ref, *, tm):
    i = pl.program_id(0)
    GD = C_GROUP_DIM
    pos = (lax.broadcasted_iota(jnp.int32, (tm, 1), 0) + i * tm + 1).astype(F32)
    for g, win in enumerate(POOL_WINDOWS):
        cur = v_ref[:, g * GD:(g + 1) * GD].astype(F32)
        halo = jnp.where(i > 0, halo_ref[:, g * GD:(g + 1) * GD].astype(F32), 0.0)
        a = jnp.concatenate([halo, cur], axis=0)
        step = 1
        while step < win:
            a = a + pltpu.roll(a, step, axis=0)
            step *= 2
        pooled = a[HALO:] / jnp.minimum(pos, float(win)) - cur
        mixed = jnp.dot(pooled.astype(BF16), w_ref[g], preferred_element_type=F32) + b_ref[g]
        gate = _silu(z_ref[:, g * GD:(g + 1) * GD].astype(F32))
        o_ref[:, g * GD:(g + 1) * GD] = (mixed * s_ref[:, g * GD:(g + 1) * GD] * gate).astype(BF16)


def _pool_mixer(p, pool_w, pool_b, pool_scale, *, tm):
    S = p.shape[0]
    tm = min(tm, S)
    W = C_GROUPS * C_GROUP_DIM
    return pl.pallas_call(
        functools.partial(_pool_kernel, tm=tm),
        grid=(S // tm,),
        in_specs=[pl.BlockSpec((tm, W), lambda i: (i, 0)),
                  pl.BlockSpec((HALO, W), lambda i: (jnp.maximum(i * (tm // HALO) - 1, 0), 0)),
                  pl.BlockSpec((tm, W), lambda i: (i, 1)),
                  pl.BlockSpec((C_GROUPS, C_GROUP_DIM, C_GROUP_DIM), lambda i: (0, 0, 0)),
                  pl.BlockSpec((C_GROUPS, 1, C_GROUP_DIM), lambda i: (0, 0, 0)),
                  pl.BlockSpec((1, W), lambda i: (0, 0))],
        out_specs=pl.BlockSpec((tm, W), lambda i: (i, 0)),
        out_shape=jax.ShapeDtypeStruct((S, W), BF16),
        compiler_params=_cparams(("arbitrary",)),
        name="pool_mixer",
    )(p, p, p, pool_w.astype(BF16), pool_b.reshape(C_GROUPS, 1, C_GROUP_DIM), pool_scale.reshape(1, W))


def _even_layer(u, w_in, conv_w, igate_b, fgate_b, head_norm_w, w_out):
    w = w_in.astype(BF16)
    gate_end = GATE_COL0 + 2 * A_HEADS
    w_attn = w[:, gate_end:]
    w_gate = jnp.pad(w[:, GATE_COL0:gate_end], ((0, 0), (0, GATE_PAD - 2 * A_HEADS)))
    col = jnp.arange(w_attn.shape[1])
    q_scale = jnp.where(col < B_WIDTH, B_HEAD_DIM ** -0.5, 1.0).astype(F32).reshape(1, -1)
    p_a = _matmul(u, w, BF16, tm=1024, tn=1024, name="in_proj_mlstm", n_cols=GATE_COL0)
    p_b = _matmul(u, w_attn, BF16, tm=1024, tn=1024, name="in_proj_attn", col_scale=q_scale)
    gates = _matmul(u, w_gate, F32, tm=1024, tn=GATE_PAD, name="gate_proj")
    gate_bias = jnp.pad(jnp.concatenate([igate_b, fgate_b]), (0, GATE_PAD - 2 * A_HEADS)).reshape(1, GATE_PAD)
    mix_a = _mlstm(p_a, gates, conv_w, gate_bias, head_norm_w, L=256)
    mix_b = _stick_breaking(p_b)
    return _matmul2(mix_a, mix_b, w_out.astype(BF16), tm=1024, tn=1024, name="out_proj_even")


def _odd_layer(u, w_in, pool_w, pool_b, pool_scale, w_out):
    p = _matmul(u, w_in.astype(BF16), BF16, tm=1024, tn=1024, name="in_proj_odd")
    h = _pool_mixer(p, pool_w, pool_b, pool_scale, tm=256)
    return _matmul(h, w_out.astype(BF16), F32, tm=1024, tn=1024, name="out_proj_odd")


def kernel(x, c, ada_w_0, ada_b_0, w_in_0, conv_w_0, igate_b_0, fgate_b_0, head_norm_w_0, w_out_0, ln_g_0, ln_b_0, ada_w_1, ada_b_1, w_in_1, pool_w_1, pool_b_1, pool_scale_1, w_out_1, ln_g_1, ln_b_1, ada_w_2, ada_b_2, w_in_2, conv_w_2, igate_b_2, fgate_b_2, head_norm_w_2, w_out_2, ln_g_2, ln_b_2, ada_w_3, ada_b_3, w_in_3, pool_w_3, pool_b_3, pool_scale_3, w_out_3, ln_g_3, ln_b_3):
    B, S, D = x.shape
    assert B == 1
    ada = [(ada_w_0, ada_b_0), (ada_w_1, ada_b_1), (ada_w_2, ada_b_2), (ada_w_3, ada_b_3)]
    post = [(ln_g_0, ln_b_0), (ln_g_1, ln_b_1), (ln_g_2, ln_b_2), (ln_g_3, ln_b_3)]
    mixers = [
        (w_in_0, conv_w_0, igate_b_0, fgate_b_0, head_norm_w_0, w_out_0),
        (w_in_1, pool_w_1, pool_b_1, pool_scale_1, w_out_1),
        (w_in_2, conv_w_2, igate_b_2, fgate_b_2, head_norm_w_2, w_out_2),
        (w_in_3, pool_w_3, pool_b_3, pool_scale_3, w_out_3),
    ]
    c_col = c.reshape(D, 1)
    mods = [_modulation(c_col, w, b) for (w, b) in ada]
    xs = x.reshape(S, D)
    u = _modulate(xs, mods[0])
    for l in range(DEPTH):
        y = _even_layer(u, *mixers[l]) if l % 2 == 0 else _odd_layer(u, *mixers[l])
        xs, u = _residual_ln(xs, y, mods[l], *post[l], mods[l + 1] if l + 1 < DEPTH else None)
    return xs.reshape(B, S, D)
```

```python
import functools

import jax
import jax.numpy as jnp
from jax import lax
from jax.experimental import pallas as pl
from jax.experimental.pallas import tpu as pltpu

F32 = jnp.float32
BF16 = jnp.bfloat16

DEPTH = 4
A_HEADS = 4
A_QK_DIM = 256
A_V_DIM = 512
A_QK = A_HEADS * A_QK_DIM
A_WIDTH = A_HEADS * A_V_DIM
CONV_WIDTH = 4
B_HEADS = 8
B_HEAD_DIM = 128
B_WIDTH = B_HEADS * B_HEAD_DIM
POOL_WINDOWS = (2, 4, 8, 16)
C_GROUPS = 4
C_GROUP_DIM = 1024
GATE_COL0 = 2 * A_QK + 3 * A_WIDTH
GATE_PAD = 128
DEEPNORM_ALPHA = (2 * DEPTH) ** 0.25
LN_EPS = 1e-5
SOFTPLUS_LINEAR_ABOVE = 20.0

LANE = 128
HALO = 16
VMEM_LIMIT = 56 * 1024 * 1024


def _cparams(sem):
    return pltpu.CompilerParams(dimension_semantics=sem, vmem_limit_bytes=VMEM_LIMIT)


def _sigmoid(x):
    return 1.0 / (1.0 + jnp.exp(-x))


def _silu(x):
    return x * _sigmoid(x)


def _log_sigmoid(x):
    return jnp.minimum(x, 0.0) - jnp.log1p(jnp.exp(-jnp.abs(x)))


def _split_bf16(x, parts):
    out = []
    r = x
    for _ in range(parts - 1):
        h = r.astype(BF16)
        out.append(h)
        r = r - h.astype(F32)
    out.append(r.astype(BF16))
    return out


def _mod_kernel(c_ref, w_ref, b_ref, o_ref):
    c = c_ref[...]
    o_ref[...] = jnp.sum(_silu(c) * w_ref[...], axis=0, keepdims=True) + b_ref[...]


def _modulation(c_col, w, b):
    D, N = w.shape
    tn = 512
    return pl.pallas_call(
        _mod_kernel,
        grid=(N // tn,),
        in_specs=[pl.BlockSpec((D, 1), lambda j: (0, 0)),
                  pl.BlockSpec((D, tn), lambda j: (0, j)),
                  pl.BlockSpec((1, tn), lambda j: (0, j))],
        out_specs=pl.BlockSpec((1, tn), lambda j: (0, j)),
        out_shape=jax.ShapeDtypeStruct((1, N), F32),
        compiler_params=_cparams(("arbitrary",)),
        name="modulation",
    )(c_col, w, b.reshape(1, N))


def _modulate_kernel(x_ref, sh_ref, sc_ref, u_ref):
    u_ref[...] = (x_ref[...] * (1.0 + sc_ref[...]) + sh_ref[...]).astype(BF16)


def _modulate(x, mod):
    S, D = x.shape
    tm = min(512, S)
    return pl.pallas_call(
        _modulate_kernel,
        grid=(S // tm,),
        in_specs=[pl.BlockSpec((tm, D), lambda i: (i, 0)),
                  pl.BlockSpec((1, D), lambda i: (0, 0)),
                  pl.BlockSpec((1, D), lambda i: (0, 1))],
        out_specs=pl.BlockSpec((tm, D), lambda i: (i, 0)),
        out_shape=jax.ShapeDtypeStruct((S, D), BF16),
        compiler_params=_cparams(("arbitrary",)),
        name="modulate",
    )(x, mod, mod)


def _mm_kernel(a_ref, b_ref, o_ref):
    o_ref[...] = jnp.dot(a_ref[...], b_ref[...], preferred_element_type=F32).astype(o_ref.dtype)


def _mm_scaled_kernel(a_ref, b_ref, s_ref, o_ref):
    acc = jnp.dot(a_ref[...], b_ref[...], preferred_element_type=F32)
    o_ref[...] = (acc * s_ref[...]).astype(o_ref.dtype)


def _matmul(a, b, out_dtype, *, tm, tn, name, n_cols=None, col_scale=None):
    M, K = a.shape
    N = b.shape[1] if n_cols is None else n_cols
    tm, tn = min(tm, M), min(tn, N)
    assert M % tm == 0 and N % tn == 0
    in_specs = [pl.BlockSpec((tm, K), lambda i, j: (i, 0)),
                pl.BlockSpec((K, tn), lambda i, j: (0, j))]
    args = [a, b]
    if col_scale is not None:
        in_specs.append(pl.BlockSpec((1, tn), lambda i, j: (0, j)))
        args.append(col_scale)
    return pl.pallas_call(
        _mm_kernel if col_scale is None else _mm_scaled_kernel,
        grid=(M // tm, N // tn),
        in_specs=in_specs,
        out_specs=pl.BlockSpec((tm, tn), lambda i, j: (i, j)),
        out_shape=jax.ShapeDtypeStruct((M, N), out_dtype),
        compiler_params=_cparams(("arbitrary", "arbitrary")),
        name=name,
    )(*args)


def _mm2_kernel(a1_ref, b1_ref, a2_ref, b2_ref, o_ref):
    o_ref[...] = (jnp.dot(a1_ref[...], b1_ref[...], preferred_element_type=F32)
                  + jnp.dot(a2_ref[...], b2_ref[...], preferred_element_type=F32))


def _matmul2(a1, a2, b, *, tm, tn, name):
    M, K1 = a1.shape
    K2 = a2.shape[1]
    N = b.shape[1]
    assert b.shape[0] == K1 + K2 and K1 % K2 == 0
    tm, tn = min(tm, M), min(tn, N)
    return pl.pallas_call(
        _mm2_kernel,
        grid=(M // tm, N // tn),
        in_specs=[pl.BlockSpec((tm, K1), lambda i, j: (i, 0)),
                  pl.BlockSpec((K1, tn), lambda i, j: (0, j)),
                  pl.BlockSpec((tm, K2), lambda i, j: (i, 0)),
                  pl.BlockSpec((K2, tn), lambda i, j: (K1 // K2, j))],
        out_specs=pl.BlockSpec((tm, tn), lambda i, j: (i, j)),
        out_shape=jax.ShapeDtypeStruct((M, N), F32),
        compiler_params=_cparams(("arbitrary", "arbitrary")),
        name=name,
    )(a1, b, a2, b)


def _ln_kernel(x_ref, y_ref, gate_ref, g_ref, b_ref, *rest, with_next):
    v = DEEPNORM_ALPHA * x_ref[...] + (1.0 + gate_ref[...]) * y_ref[...]
    mu = jnp.mean(v, axis=-1, keepdims=True)
    d = v - mu
    var = jnp.mean(d * d, axis=-1, keepdims=True)
    xn = d * lax.rsqrt(var + LN_EPS) * g_ref[...] + b_ref[...]
    if with_next:
        sh_ref, sc_ref, xo_ref, u_ref = rest
        u_ref[...] = (xn * (1.0 + sc_ref[...]) + sh_ref[...]).astype(BF16)
    else:
        (xo_ref,) = rest
    xo_ref[...] = xn


def _residual_ln(x, y, mod, ln_g, ln_b, mod_next):
    S, D = x.shape
    tm = min(256, S)
    row = pl.BlockSpec((tm, D), lambda i: (i, 0))

    def vec(k):
        return pl.BlockSpec((1, D), lambda i: (0, k))

    in_specs = [row, row, vec(2), vec(0), vec(0)]
    args = [x, y, mod, ln_g.reshape(1, D), ln_b.reshape(1, D)]
    out_specs = [row]
    out_shape = [jax.ShapeDtypeStruct((S, D), F32)]
    if mod_next is not None:
        in_specs += [vec(0), vec(1)]
        args += [mod_next, mod_next]
        out_specs.append(row)
        out_shape.append(jax.ShapeDtypeStruct((S, D), BF16))
    res = pl.pallas_call(
        functools.partial(_ln_kernel, with_next=mod_next is not None),
        grid=(S // tm,),
        in_specs=in_specs,
        out_specs=out_specs,
        out_shape=out_shape,
        compiler_params=_cparams(("arbitrary",)),
        name="residual_ln",
    )(*args)
    return (res[0], res[1]) if mod_next is not None else (res[0], None)


def _mlstm_kernel(qk_ref, halo_ref, v_ref, o_ref, z_ref, g_ref, conv_ref, gb_ref, hn_ref,
                  out_ref, C_ref, n_ref, m_ref, *, L):
    H, DK, DV = A_HEADS, A_QK_DIM, A_V_DIM
    c = pl.program_id(0)

    @pl.when(c == 0)
    def _():
        C_ref[...] = jnp.zeros_like(C_ref)
        n_ref[...] = jnp.zeros_like(n_ref)
        m_ref[...] = jnp.zeros_like(m_ref)

    cur = qk_ref[...].astype(F32)
    halo = jnp.where(c > 0, halo_ref[...].astype(F32), 0.0)
    xp = jnp.concatenate([halo, cur], axis=0)
    cw = conv_ref[...]
    acc = cw[CONV_WIDTH - 1:CONV_WIDTH] * cur
    for j in range(CONV_WIDTH - 1):
        shifted = pltpu.roll(xp, CONV_WIDTH - 1 - j, axis=0)[HALO:]
        acc = acc + cw[j:j + 1] * shifted
    qk = _silu(acc)

    gb = g_ref[...] + gb_ref[...]
    lane = lax.broadcasted_iota(jnp.int32, gb.shape, 1)
    gcol = jnp.where(lane < H, gb, _log_sigmoid(gb))
    grow = gcol.T[:2 * H]
    r_i = lax.broadcasted_iota(jnp.int32, (L, L), 0)
    c_i = lax.broadcasted_iota(jnp.int32, (L, L), 1)
    causal = c_i <= r_i
    tri_lo = jnp.where(causal, 1.0, 0.0).astype(BF16)
    tri_up = jnp.where(c_i >= r_i, 1.0, 0.0).astype(BF16)
    bcol = sum(jnp.dot(tri_lo, p, preferred_element_type=F32) for p in _split_bf16(gcol, 3))
    brow = sum(jnp.dot(p, tri_up, preferred_element_type=F32) for p in _split_bf16(grow, 3))

    for h in range(H):
        li_row = grow[h:h + 1, :]
        li_col = gcol[:, h:h + 1]
        b_row = brow[H + h:H + h + 1, :]
        b_col = bcol[:, H + h:H + h + 1]
        m_prev = m_ref[h][0:1, 0:1]
        q = qk[:, h * DK:(h + 1) * DK] * (DK ** -0.5)
        k = qk[:, (H + h) * DK:(H + h + 1) * DK]
        vh = v_ref[:, h * DV:(h + 1) * DV]
        qb = q.astype(BF16)

        dmat = jnp.where(causal, b_col - b_row + li_row, -jnp.inf)
        inter = b_col + m_prev
        m_t = jnp.maximum(inter, jnp.max(dmat, axis=1, keepdims=True))
        w_intra = jnp.exp(dmat - m_t)
        s = lax.dot_general(qb, k.astype(BF16), (((1,), (1,)), ((), ())),
                            preferred_element_type=F32) * w_intra
        w_inter = jnp.exp(inter - m_t)
        num = (jnp.dot(s.astype(BF16), vh, preferred_element_type=F32)
               + w_inter * jnp.dot(qb, C_ref[h].astype(BF16), preferred_element_type=F32))
        den = (jnp.sum(s, axis=1, keepdims=True)
               + w_inter * jnp.sum(q * n_ref[h], axis=1, keepdims=True))
        den = jnp.maximum(jnp.abs(den), jnp.exp(-m_t))
        hh = num / den

        b_last = b_col[L - 1:L, :]
        wdec_row = b_last - b_row + li_row
        wdec_col = b_last - b_col + li_col
        m_new = jnp.maximum(b_last + m_prev, jnp.max(wdec_row, axis=1, keepdims=True))
        kw = k * jnp.exp(wdec_col - m_new)
        decay = jnp.exp(b_last + m_prev - m_new)
        C_ref[h] = decay * C_ref[h] + lax.dot_general(
            kw.astype(BF16), vh, (((0,), (0,)), ((), ())), preferred_element_type=F32)
        n_ref[h] = decay * n_ref[h] + jnp.sum(kw, axis=0, keepdims=True)
        m_ref[h] = jnp.broadcast_to(m_new, m_ref.shape[1:])

        og = _sigmoid(o_ref[:, h * DV:(h + 1) * DV].astype(F32)) * hh
        mu = jnp.mean(og, axis=-1, keepdims=True)
        d = og - mu
        var = jnp.mean(d * d, axis=-1, keepdims=True)
        hn = d * lax.rsqrt(var + LN_EPS) * hn_ref[:, h * DV:(h + 1) * DV]
        out_ref[:, h * DV:(h + 1) * DV] = (
            hn * _silu(z_ref[:, h * DV:(h + 1) * DV].astype(F32))).astype(BF16)


def _mlstm(p, gates, conv_w, gate_bias, head_norm_w, *, L):
    S = p.shape[0]
    L = min(L, S)
    QK2 = 2 * A_QK
    blk = lambda col: pl.BlockSpec((L, A_WIDTH), lambda c: (c, col))
    return pl.pallas_call(
        functools.partial(_mlstm_kernel, L=L),
        grid=(S // L,),
        in_specs=[pl.BlockSpec((L, QK2), lambda c: (c, 0)),
                  pl.BlockSpec((HALO, QK2), lambda c: (jnp.maximum(c * (L // HALO) - 1, 0), 0)),
                  blk(1), blk(2), blk(3),
                  pl.BlockSpec((L, GATE_PAD), lambda c: (c, 0)),
                  pl.BlockSpec((CONV_WIDTH, QK2), lambda c: (0, 0)),
                  pl.BlockSpec((1, GATE_PAD), lambda c: (0, 0)),
                  pl.BlockSpec((1, A_WIDTH), lambda c: (0, 0))],
        out_specs=pl.BlockSpec((L, A_WIDTH), lambda c: (c, 0)),
        out_shape=jax.ShapeDtypeStruct((S, A_WIDTH), BF16),
        scratch_shapes=[pltpu.VMEM((A_HEADS, A_QK_DIM, A_V_DIM), F32),
                        pltpu.VMEM((A_HEADS, 1, A_QK_DIM), F32),
                        pltpu.VMEM((A_HEADS, 8, LANE), F32)],
        compiler_params=_cparams(("arbitrary",)),
        name="mlstm",
    )(p, p, p, p, p, gates, conv_w, gate_bias, head_norm_w.reshape(1, A_WIDTH))


ATTN_HEADS_PER_STEP = 2
ATTN_TQ = 512
ATTN_TK = 128


def _attn_kernel(q_ref, k_ref, v_ref, z_ref, o_ref, acc_ref, carry_ref, score_ref, p_ref):
    i = pl.program_id(1)
    dh, NH, TQ, TK = B_HEAD_DIM, ATTN_HEADS_PER_STEP, ATTN_TQ, ATTN_TK
    TH = TQ // 2
    NSUB = TH // TK
    s_i = lax.broadcasted_iota(jnp.int32, (TK, TH), 0)
    t_i = lax.broadcasted_iota(jnp.int32, (TK, TH), 1)
    r_i = lax.broadcasted_iota(jnp.int32, (TK, TK), 0)
    j_i = lax.broadcasted_iota(jnp.int32, (TK, TK), 1)
    tri = jnp.where(j_i >= r_i, 1.0, 0.0).astype(BF16)
    q_t = [q_ref[:, n * dh:(n + 1) * dh].astype(F32).T.astype(BF16) for n in range(NH)]

    acc_ref[...] = jnp.zeros_like(acc_ref)
    carry_ref[...] = jnp.zeros_like(carry_ref)

    def softplus_parts(z, mask):
        sp = jnp.where(z > SOFTPLUS_LINEAR_ABOVE, z, jnp.log(1.0 + jnp.exp(z)))
        if mask is not None:
            sp = jnp.where(mask, sp, 0.0)
        return sp.astype(BF16)

    def diag_block(start, hf, own):
        chains = [(n, b) for n in range(NH) for b in reversed(range(NSUB))]
        mask = {b: (s_i + b * TK) < t_i if own else None for b in range(NSUB)}
        zs, parts, sums = {}, {}, {}
        for (n, b) in chains:
            kb = k_ref[pl.ds(start + b * TK, TK), n * dh:(n + 1) * dh]
            zs[n, b] = jnp.dot(kb, q_t[n][:, hf * TH:(hf + 1) * TH], preferred_element_type=F32)
        for (n, b) in chains:
            parts[n, b] = softplus_parts(zs[n, b], mask[b])
        for c in chains:
            sums[c] = jnp.dot(tri, parts[c], preferred_element_type=F32)
        for n in range(NH):
            carry = carry_ref[n, hf]
            ps = [None] * NSUB
            for b in reversed(range(NSUB)):
                p = jnp.exp(zs[n, b] - sums[n, b] - carry)
                if own:
                    p = jnp.where(mask[b], p, 0.0)
                carry = carry + sums[n, b][0:1, :]
                ps[b] = p.astype(BF16)
            carry_ref[n, hf] = carry
            vb = v_ref[pl.ds(start, TH), n * dh:(n + 1) * dh]
            acc_ref[n, hf] += lax.dot_general(vb, jnp.concatenate(ps, axis=0), (((0,), (0,)), ((), ())),
                                              preferred_element_type=F32)

    def scores(start, slot):
        for n in range(NH):
            for b in range(NSUB):
                kb = k_ref[pl.ds(start + b * TK, TK), n * dh:(n + 1) * dh]
                score_ref[slot, n, b] = jnp.dot(kb, q_t[n], preferred_element_type=F32)

    def weighted_values(start):
        for n in range(NH):
            vb = v_ref[pl.ds(start, TH), n * dh:(n + 1) * dh]
            for hf in range(2):
                acc_ref[n, hf] += lax.dot_general(vb, p_ref[n, hf], (((0,), (0,)), ((), ())),
                                                  preferred_element_type=F32)

    def loop_block(start, slot):
        chains = [(n, hf, b) for n in range(NH) for hf in range(2) for b in reversed(range(NSUB))]

        def z_of(c):
            n, hf, b = c
            return score_ref[slot, n, b, :, hf * TH:(hf + 1) * TH]

        scores(pl.multiple_of(jnp.maximum(start - TH, 0), TH), 1 - slot)
        weighted_values(pl.multiple_of(start + TH, TH))
        parts, sums = {}, {}
        for c in chains:
            parts[c] = softplus_parts(z_of(c), None)
        for c in chains:
            sums[c] = jnp.dot(tri, parts[c], preferred_element_type=F32)
        for n in range(NH):
            for hf in range(2):
                carry = carry_ref[n, hf]
                for b in reversed(range(NSUB)):
                    p = jnp.exp(z_of((n, hf, b)) - sums[n, hf, b] - carry)
                    carry = carry + sums[n, hf, b][0:1, :]
                    p_ref[n, hf, b * TK:(b + 1) * TK, :] = p.astype(BF16)
                carry_ref[n, hf] = carry

    def body(jj, _):
        start = pl.multiple_of(base - (2 * jj + 1) * TH, TH)
        loop_block(start, 0)
        loop_block(pl.multiple_of(start - TH, TH), 1)
        return 0

    base = pl.multiple_of(i * TQ, TQ)
    diag_block(base + TH, 1, True)
    diag_block(base, 1, False)
    diag_block(base, 0, True)
    scores(pl.multiple_of(jnp.maximum(base - TH, 0), TH), 0)
    p_ref[...] = jnp.zeros_like(p_ref)
    lax.fori_loop(0, i, body, 0)
    weighted_values(0)
    for n in range(NH):
        for hf in range(2):
            gate = _silu(z_ref[hf * TH:(hf + 1) * TH, n * dh:(n + 1) * dh].astype(F32))
            o_ref[hf * TH:(hf + 1) * TH, n * dh:(n + 1) * dh] = (acc_ref[n, hf].T * gate).astype(BF16)


def _stick_breaking(p):
    S = p.shape[0]
    NH, TQ, dh = ATTN_HEADS_PER_STEP, ATTN_TQ, B_HEAD_DIM
    assert S % TQ == 0
    W = NH * dh
    nb = B_WIDTH // W
    return pl.pallas_call(
        _attn_kernel,
        grid=(nb, S // TQ),
        in_specs=[pl.BlockSpec((TQ, W), lambda h, i: (i, h)),
                  pl.BlockSpec((S, W), lambda h, i: (0, nb + h)),
                  pl.BlockSpec((S, W), lambda h, i: (0, 2 * nb + h)),
                  pl.BlockSpec((TQ, W), lambda h, i: (i, 3 * nb + h))],
        out_specs=pl.BlockSpec((TQ, W), lambda h, i: (i, h)),
        out_shape=jax.ShapeDtypeStruct((S, B_WIDTH), BF16),
        scratch_shapes=[pltpu.VMEM((NH, 2, dh, TQ // 2), F32),
                        pltpu.VMEM((NH, 2, 1, TQ // 2), F32),
                        pltpu.VMEM((2, NH, (TQ // 2) // ATTN_TK, ATTN_TK, TQ), F32),
                        pltpu.VMEM((NH, 2, TQ // 2, TQ // 2), BF16)],
        compiler_params=_cparams(("arbitrary", "arbitrary")),
        name="stick_breaking",
    )(p, p, p, p)


def _pool_kernel(v_ref, halo_ref, z_ref, w_ref, b_ref, s_ref, o_ref, *, tm):
    i = pl.program_id(0)
    GD = C_GROUP_DIM
    pos = (lax.broadcasted_iota(jnp.int32, (tm, 1), 0) + i * tm + 1).astype(F32)
    for g, win in enumerate(POOL_WINDOWS):
        cur = v_ref[:, g * GD:(g + 1) * GD].astype(F32)
        halo = jnp.where(i > 0, halo_ref[:, g * GD:(g + 1) * GD].astype(F32), 0.0)
        a = jnp.concatenate([halo, cur], axis=0)
        step = 1
        while step < win:
            a = a + pltpu.roll(a, step, axis=0)
            step *= 2
        pooled = a[HALO:] / jnp.minimum(pos, float(win)) - cur
        mixed = jnp.dot(pooled.astype(BF16), w_ref[g], preferred_element_type=F32) + b_ref[g]
        gate = _silu(z_ref[:, g * GD:(g + 1) * GD].astype(F32))
        o_ref[:, g * GD:(g + 1) * GD] = (mixed * s_ref[:, g * GD:(g + 1) * GD] * gate).astype(BF16)


def _pool_mixer(p, pool_w, pool_b, pool_scale, *, tm):
    S = p.shape[0]
    tm = min(tm, S)
    W = C_GROUPS * C_GROUP_DIM
    return pl.pallas_call(
        functools.partial(_pool_kernel, tm=tm),
        grid=(S // tm,),
        in_specs=[pl.BlockSpec((tm, W), lambda i: (i, 0)),
                  pl.BlockSpec((HALO, W), lambda i: (jnp.maximum(i * (tm // HALO) - 1, 0), 0)),
                  pl.BlockSpec((tm, W), lambda i: (i, 1)),
                  pl.BlockSpec((C_GROUPS, C_GROUP_DIM, C_GROUP_DIM), lambda i: (0, 0, 0)),
                  pl.BlockSpec((C_GROUPS, 1, C_GROUP_DIM), lambda i: (0, 0, 0)),
                  pl.BlockSpec((1, W), lambda i: (0, 0))],
        out_specs=pl.BlockSpec((tm, W), lambda i: (i, 0)),
        out_shape=jax.ShapeDtypeStruct((S, W), BF16),
        compiler_params=_cparams(("arbitrary",)),
        name="pool_mixer",
    )(p, p, p, pool_w.astype(BF16), pool_b.reshape(C_GROUPS, 1, C_GROUP_DIM), pool_scale.reshape(1, W))


def _even_layer(u, w_in, conv_w, igate_b, fgate_b, head_norm_w, w_out):
    w = w_in.astype(BF16)
    gate_end = GATE_COL0 + 2 * A_HEADS
    w_attn = w[:, gate_end:]
    w_gate = jnp.pad(w[:, GATE_COL0:gate_end], ((0, 0), (0, GATE_PAD - 2 * A_HEADS)))
    col = jnp.arange(w_attn.shape[1])
    q_scale = jnp.where(col < B_WIDTH, B_HEAD_DIM ** -0.5, 1.0).astype(F32).reshape(1, -1)
    p_a = _matmul(u, w, BF16, tm=1024, tn=1024, name="in_proj_mlstm", n_cols=GATE_COL0)
    p_b = _matmul(u, w_attn, BF16, tm=1024, tn=1024, name="in_proj_attn", col_scale=q_scale)
    gates = _matmul(u, w_gate, F32, tm=1024, tn=GATE_PAD, name="gate_proj")
    gate_bias = jnp.pad(jnp.concatenate([igate_b, fgate_b]), (0, GATE_PAD - 2 * A_HEADS)).reshape(1, GATE_PAD)
    mix_a = _mlstm(p_a, gates, conv_w, gate_bias, head_norm_w, L=256)
    mix_b = _stick_breaking(p_b)
    return _matmul2(mix_a, mix_b, w_out.astype(BF16), tm=1024, tn=1024, name="out_proj_even")


def _odd_layer(u, w_in, pool_w, pool_b, pool_scale, w_out):
    p = _matmul(u, w_in.astype(BF16), BF16, tm=1024, tn=1024, name="in_proj_odd")
    h = _pool_mixer(p, pool_w, pool_b, pool_scale, tm=256)
    return _matmul(h, w_out.astype(BF16), F32, tm=1024, tn=1024, name="out_proj_odd")


def kernel(x, c, ada_w_0, ada_b_0, w_in_0, conv_w_0, igate_b_0, fgate_b_0, head_norm_w_0, w_out_0, ln_g_0, ln_b_0, ada_w_1, ada_b_1, w_in_1, pool_w_1, pool_b_1, pool_scale_1, w_out_1, ln_g_1, ln_b_1, ada_w_2, ada_b_2, w_in_2, conv_w_2, igate_b_2, fgate_b_2, head_norm_w_2, w_out_2, ln_g_2, ln_b_2, ada_w_3, ada_b_3, w_in_3, pool_w_3, pool_b_3, pool_scale_3, w_out_3, ln_g_3, ln_b_3):
    B, S, D = x.shape
    assert B == 1
    ada = [(ada_w_0, ada_b_0), (ada_w_1, ada_b_1), (ada_w_2, ada_b_2), (ada_w_3, ada_b_3)]
    post = [(ln_g_0, ln_b_0), (ln_g_1, ln_b_1), (ln_g_2, ln_b_2), (ln_g_3, ln_b_3)]
    mixers = [
        (w_in_0, conv_w_0, igate_b_0, fgate_b_0, head_norm_w_0, w_out_0),
        (w_in_1, pool_w_1, pool_b_1, pool_scale_1, w_out_1),
        (w_in_2, conv_w_2, igate_b_2, fgate_b_2, head_norm_w_2, w_out_2),
        (w_in_3, pool_w_3, pool_b_3, pool_scale_3, w_out_3),
    ]
    c_col = c.reshape(D, 1)
    mods = [_modulation(c_col, w, b) for (w, b) in ada]
    xs = x.reshape(S, D)
    u = _modulate(xs, mods[0])
    for l in range(DEPTH):
        y = _even_layer(u, *mixers[l]) if l % 2 == 0 else _odd_layer(u, *mixers[l])
        xs, u = _residual_ln(xs, y, mods[l], *post[l], mods[l + 1] if l + 1 < DEPTH else None)
    return xs.reshape(B, S, D)
```
